```python
import math
import jax, jax.numpy as jnp
from jax import lax
import numpy as np

D_MODEL = 2048
BATCH = 4
SEQ = 4096
DEPTH = 4

N_EVEN = (DEPTH + 1) // 2
N_ODD = DEPTH // 2
EPS = 1e-6
NEG_INF = -1e30

S5_WIDTH = 1024
S5_GROUP = 16
S5_GROUPS = S5_WIDTH // S5_GROUP
S5_STATE = 64
S5_DT_MIN = 1e-3
S5_DT_MAX = 1e-1

DIL_PATTERNS = ((128, 1), (512, 4), (2048, 16))
DIL_HEADS = 4
DIL_HEAD_DIM = 128
DIL_WIDTH = DIL_HEADS * DIL_HEAD_DIM

EVEN_IN = S5_WIDTH + 3 * len(DIL_PATTERNS) * DIL_WIDTH
EVEN_OUT = S5_WIDTH + DIL_WIDTH

CONV_WIDTH = 1024
CONV_KERNEL = 31

GDN_HEADS = 8
GDN_HEAD_DIM = 128
GDN_WIDTH = GDN_HEADS * GDN_HEAD_DIM
GDN_CONV = 4
GDN_CHUNK = 64

ODD_IN = 2 * CONV_WIDTH + 4 * GDN_WIDTH + 2 * GDN_HEADS
ODD_OUT = CONV_WIDTH + GDN_WIDTH

D_FF = 5632
N_EXPERTS = 8
TOP_K = 2
D_FF_EXPERT = 2816

kernel_name = "hybrid_s5_dilattn_conformer_gdn_moe"


def rmsnorm(x, g):
    xf = x.astype(jnp.float32)
    y = xf * lax.rsqrt(jnp.mean(xf * xf, axis=-1, keepdims=True) + EPS) * g.astype(jnp.float32)
    return y.astype(x.dtype)


def layernorm(x, g, b):
    xf = x.astype(jnp.float32)
    mu = jnp.mean(xf, axis=-1, keepdims=True)
    var = jnp.mean(jnp.square(xf - mu), axis=-1, keepdims=True)
    y = (xf - mu) * lax.rsqrt(var + EPS) * g.astype(jnp.float32) + b.astype(jnp.float32)
    return y.astype(x.dtype)


def l2norm(x):
    return x * lax.rsqrt(jnp.sum(x * x, axis=-1, keepdims=True) + EPS)


def causal_depthwise_conv(x, w):
    k, ch = w.shape
    return lax.conv_general_dilated(
        x, w.astype(x.dtype)[:, None, :], window_strides=(1,), padding=[(k - 1, 0)],
        dimension_numbers=('NWC', 'WIO', 'NWC'), feature_group_count=ch)


def swiglu(h, w_gate, w_up, w_down):
    return (jax.nn.silu(h @ w_gate) * (h @ w_up)) @ w_down


def _complex_scan_combine(left, right):
    ar1, ai1, br1, bi1 = left
    ar2, ai2, br2, bi2 = right
    return (ar2 * ar1 - ai2 * ai1,
            ar2 * ai1 + ai2 * ar1,
            ar2 * br1 - ai2 * bi1 + br2,
            ar2 * bi1 + ai2 * br1 + bi2)


def s5_mixer(u, log_dt, lam_re, lam_im, b_re, b_im, c_re, c_im, d_skip, glu_w, glu_b):
    bsz, seq, _ = u.shape
    f32 = jnp.float32
    uf = u.astype(f32)
    ug = uf.reshape(bsz, seq, S5_GROUPS, S5_GROUP)
    dt = jnp.exp(log_dt.astype(f32))[:, None]
    lr, li = lam_re.astype(f32), lam_im.astype(f32)
    mag = jnp.exp(lr * dt)
    ab_re, ab_im = mag * jnp.cos(li * dt), mag * jnp.sin(li * dt)
    nr, ni = ab_re - 1.0, ab_im
    den = lr * lr + li * li
    zr = (nr * lr + ni * li) / den
    zi = (ni * lr - nr * li) / den
    br, bi = b_re.astype(f32), b_im.astype(f32)
    bbar_re = zr[..., None] * br - zi[..., None] * bi
    bbar_im = zr[..., None] * bi + zi[..., None] * br
    bu_re = jnp.einsum('blgh,gph->blgp', ug, bbar_re)
    bu_im = jnp.einsum('blgh,gph->blgp', ug, bbar_im)
    a_re = jnp.broadcast_to(ab_re, (1, seq, S5_GROUPS, S5_STATE))
    a_im = jnp.broadcast_to(ab_im, (1, seq, S5_GROUPS, S5_STATE))
    _, _, xr, xi = lax.associative_scan(_complex_scan_combine, (a_re, a_im, bu_re, bu_im), axis=1)
    y = (jnp.einsum('blgp,ghp->blgh', xr, c_re.astype(f32))
         - jnp.einsum('blgp,ghp->blgh', xi, c_im.astype(f32)))
    y = y.reshape(bsz, seq, S5_WIDTH) + d_skip.astype(f32) * uf
    y = jax.nn.gelu(y)
    y = y * jax.nn.sigmoid(y @ glu_w.astype(f32) + glu_b.astype(f32))
    return y.astype(u.dtype)


def dilated_branch(q, k, v, window, dilation):
    bsz, seq, nh, hd = q.shape
    blk = window // dilation
    n = seq // dilation
    nb = -(-n // blk)
    n_pad = nb * blk

    def to_blocks(t):
        t = t.reshape(bsz, n, dilation, nh, hd).transpose(0, 2, 1, 3, 4)
        t = jnp.pad(t, ((0, 0), (0, 0), (0, n_pad - n), (0, 0), (0, 0)))
        return t.reshape(bsz, dilation, nb, blk, nh, hd)

    def with_prev(t):
        prev = jnp.pad(t[:, :, :-1], ((0, 0), (0, 0), (1, 0), (0, 0), (0, 0), (0, 0)))
        return jnp.concatenate([prev, t], axis=3)

    qb = to_blocks(q)
    kw, vw = with_prev(to_blocks(k)), with_prev(to_blocks(v))
    starts = jnp.arange(nb)[:, None, None] * blk
    qpos = starts + jnp.arange(blk)[None, :, None]
    kpos = starts - blk + jnp.arange(2 * blk)[None, None, :]
    dist = qpos - kpos
    mask = (dist >= 0) & (dist <= blk) & (kpos >= 0)
    s = jnp.einsum('bdnqhe,bdnkhe->bdnhqk', qb, kw) * (hd ** -0.5)
    s = jnp.where(mask[:, None], s, NEG_INF)
    m = jnp.max(s, axis=-1, keepdims=True)
    p = jnp.exp(s - m)
    l = jnp.sum(p, axis=-1, keepdims=True)
    o = jnp.einsum('bdnhqk,bdnkhe->bdnqhe', p / l, vw)
    lse = jnp.swapaxes((m + jnp.log(l))[..., 0], 3, 4)

    def from_blocks(t):
        t = t.reshape(bsz, dilation, n_pad, *t.shape[4:])[:, :, :n]
        t = jnp.swapaxes(t, 1, 2)
        return t.reshape(bsz, seq, *t.shape[3:])

    return from_blocks(o), from_blocks(lse)


def dilated_attention(qkv):
    bsz, seq, _ = qkv.shape
    qkv_f = qkv.astype(jnp.float32).reshape(bsz, seq, len(DIL_PATTERNS), 3, DIL_HEADS, DIL_HEAD_DIM)
    outs, lses = [], []
    for g, (window, dilation) in enumerate(DIL_PATTERNS):
        o, lse = dilated_branch(qkv_f[:, :, g, 0], qkv_f[:, :, g, 1], qkv_f[:, :, g, 2], window, dilation)
        outs.append(o)
        lses.append(lse)
    wts = jax.nn.softmax(jnp.stack(lses, axis=0), axis=0)
    out = jnp.sum(wts[..., None] * jnp.stack(outs, axis=0), axis=0)
    return out.reshape(bsz, seq, DIL_WIDTH).astype(qkv.dtype)


def conformer_conv(h, dw, dw_b, ln_g, ln_b):
    a, gate = jnp.split(h, 2, axis=-1)
    y = a * jax.nn.sigmoid(gate)
    y = causal_depthwise_conv(y, dw) + dw_b.astype(y.dtype)
    y = layernorm(y, ln_g, ln_b)
    return jax.nn.silu(y)


def chunk_gated_delta_rule(q, k, v, g, beta):
    bsz, seq, nh, dk = q.shape
    dv = v.shape[-1]
    cs = GDN_CHUNK
    nc = seq // cs

    def chunks(t):
        t = jnp.swapaxes(t, 1, 2)
        return t.reshape(bsz, nh, nc, cs, *t.shape[3:])

    q, k, v, g, beta = chunks(q), chunks(k), chunks(v), chunks(g), chunks(beta)
    g = jnp.cumsum(g, axis=-1)
    causal = jnp.tril(jnp.ones((cs, cs), dtype=bool))
    strict = jnp.tril(jnp.ones((cs, cs), dtype=bool), -1)
    decay = jnp.exp(jnp.where(causal, g[..., :, None] - g[..., None, :], -jnp.inf))
    k_beta = k * beta[..., None]
    v_beta = v * beta[..., None]
    lower = jnp.where(strict, jnp.einsum('bhnid,bhnjd->bhnij', k_beta, k) * decay, 0.0)
    tri = lower + jnp.eye(cs, dtype=lower.dtype)
    rhs = jnp.concatenate([v_beta, k_beta * jnp.exp(g)[..., None]], axis=-1)
    sol = lax.linalg.triangular_solve(tri, rhs, left_side=True, lower=True)
    u, w = sol[..., :dv], sol[..., dv:]
    attn = jnp.einsum('bhnid,bhnjd->bhnij', q, k) * decay

    def step(state, inp):
        q_c, k_c, u_c, w_c, g_c, a_c = inp
        v_new = u_c - jnp.einsum('bhcd,bhde->bhce', w_c, state)
        o = (jnp.einsum('bhcd,bhde->bhce', q_c * jnp.exp(g_c)[..., None], state)
             + jnp.einsum('bhij,bhje->bhie', a_c, v_new))
        g_last = g_c[..., -1]
        k_dec = k_c * jnp.exp(g_last[..., None] - g_c)[..., None]
        state = state * jnp.exp(g_last)[..., None, None] + jnp.einsum('bhcd,bhce->bhde', k_dec, v_new)
        return state, o

    xs = tuple(jnp.moveaxis(t, 2, 0) for t in (q, k, u, w, g, attn))
    state0 = jnp.zeros((bsz, nh, dk, dv), jnp.float32)
    _, o = lax.scan(step, state0, xs)
    o = jnp.transpose(o, (1, 0, 3, 2, 4))
    return o.reshape(bsz, seq, nh, dv)


def gated_deltanet(h, conv_w, a_log, dt_bias, norm_g):
    bsz, seq, _ = h.shape
    f32 = jnp.float32
    qkv, z, b, a = jnp.split(h, [3 * GDN_WIDTH, 4 * GDN_WIDTH, 4 * GDN_WIDTH + GDN_HEADS], axis=-1)
    qkv = jax.nn.silu(causal_depthwise_conv(qkv, conv_w)).astype(f32)
    qkv = qkv.reshape(bsz, seq, 3, GDN_HEADS, GDN_HEAD_DIM)
    q = l2norm(qkv[:, :, 0]) * (GDN_HEAD_DIM ** -0.5)
    k = l2norm(qkv[:, :, 1])
    v = qkv[:, :, 2]
    beta = jax.nn.sigmoid(b.astype(f32))
    g = -jnp.exp(a_log.astype(f32)) * jax.nn.softplus(a.astype(f32) + dt_bias.astype(f32))
    o = chunk_gated_delta_rule(q, k, v, g, beta)
    o = rmsnorm(o, norm_g) * jax.nn.silu(z.astype(f32).reshape(bsz, seq, GDN_HEADS, GDN_HEAD_DIM))
    return o.reshape(bsz, seq, GDN_WIDTH).astype(h.dtype)


def moe_ffn(h, router, w_gate, w_up, w_down):
    bsz, seq, d = h.shape
    t = h.reshape(bsz * seq, d)
    logits = (t @ router).astype(jnp.float32)
    top_val, top_idx = lax.top_k(logits, TOP_K)
    top_w = jax.nn.softmax(top_val, axis=-1)
    gates = jnp.sum(jax.nn.one_hot(top_idx, N_EXPERTS, dtype=jnp.float32) * top_w[..., None], axis=1)
    out = jnp.zeros_like(t)
    for e in range(N_EXPERTS):
        out = out + gates[:, e:e + 1].astype(t.dtype) * swiglu(t, w_gate[e], w_up[e], w_down[e])
    return out.reshape(bsz, seq, d)


def setup_inputs(seed: int = 0) -> dict:
    key = jax.random.key(seed)
    ks = iter(jax.random.split(key, 40))
    f32 = jnp.float32
    D = D_MODEL

    def nrm(shape, scale):
        return jax.random.normal(next(ks), shape, f32) * scale

    def unif(shape, lo, hi):
        return jax.random.uniform(next(ks), shape, f32, lo, hi)

    x = nrm((BATCH, SEQ, D), 1.0)
    c = nrm((BATCH, D), 1.0)
    ada_w = nrm((DEPTH, D, 6 * D), 0.5 * D ** -0.5)
    ada_b = nrm((DEPTH, 6 * D), 0.01)
    norm1_g = 1.0 + nrm((DEPTH, D), 0.02)
    norm2_g = 1.0 + nrm((DEPTH, D), 0.02)
    ev_w_in = nrm((N_EVEN, D, EVEN_IN), D ** -0.5)
    s5_log_dt = unif((N_EVEN, S5_GROUPS), math.log(S5_DT_MIN), math.log(S5_DT_MAX))
    s5_lambda_re = -0.5 * jnp.exp(nrm((N_EVEN, S5_GROUPS, S5_STATE), 0.05))
    s5_lambda_im = math.pi * jnp.arange(S5_STATE, dtype=f32) + nrm((N_EVEN, S5_GROUPS, S5_STATE), 0.05)
    s5_b_re = nrm((N_EVEN, S5_GROUPS, S5_STATE, S5_GROUP), (2 * S5_GROUP) ** -0.5)
    s5_b_im = nrm((N_EVEN, S5_GROUPS, S5_STATE, S5_GROUP), (2 * S5_GROUP) ** -0.5)
    s5_c_re = nrm((N_EVEN, S5_GROUPS, S5_GROUP, S5_STATE), S5_STATE ** -0.5)
    s5_c_im = nrm((N_EVEN, S5_GROUPS, S5_GROUP, S5_STATE), S5_STATE ** -0.5)
    s5_d = nrm((N_EVEN, S5_WIDTH), 1.0)
    s5_glu_w = nrm((N_EVEN, S5_WIDTH, S5_WIDTH), S5_WIDTH ** -0.5)
    s5_glu_b = nrm((N_EVEN, S5_WIDTH), 0.01)
    ev_w_out = nrm((N_EVEN, EVEN_OUT, D), EVEN_OUT ** -0.5)
    od_w_in = nrm((N_ODD, D, ODD_IN), D ** -0.5)
    conf_dw = nrm((N_ODD, CONV_KERNEL, CONV_WIDTH), CONV_KERNEL ** -0.5)
    conf_dw_b = nrm((N_ODD, CONV_WIDTH), 0.01)
    conf_ln_g = 1.0 + nrm((N_ODD, CONV_WIDTH), 0.02)
    conf_ln_b = nrm((N_ODD, CONV_WIDTH), 0.01)
    gdn_conv = nrm((N_ODD, GDN_CONV, 3 * GDN_WIDTH), GDN_CONV ** -0.5)
    gdn_a_log = jnp.log(unif((N_ODD, GDN_HEADS), 1.0, 16.0))
    dt = jnp.exp(unif((N_ODD, GDN_HEADS), math.log(1e-3), math.log(1e-1)))
    gdn_dt_bias = dt + jnp.log(-jnp.expm1(-dt))
    gdn_norm_g = 1.0 + nrm((N_ODD, GDN_HEAD_DIM), 0.02)
    od_w_out = nrm((N_ODD, ODD_OUT, D), ODD_OUT ** -0.5)
    ffn_w_gate = nrm((N_EVEN, D, D_FF), D ** -0.5)
    ffn_w_up = nrm((N_EVEN, D, D_FF), D ** -0.5)
    ffn_w_down = nrm((N_EVEN, D_FF, D), D_FF ** -0.5)
    moe_router = nrm((N_ODD, D, N_EXPERTS), D ** -0.5)
    moe_w_gate = nrm((N_ODD, N_EXPERTS, D, D_FF_EXPERT), D ** -0.5)
    moe_w_up = nrm((N_ODD, N_EXPERTS, D, D_FF_EXPERT), D ** -0.5)
    moe_w_down = nrm((N_ODD, N_EXPERTS, D_FF_EXPERT, D), D_FF_EXPERT ** -0.5)
    final_norm_g = 1.0 + nrm((D,), 0.02)
    return {
        'x': x, 'c': c, 'ada_w': ada_w, 'ada_b': ada_b, 'norm1_g': norm1_g, 'norm2_g': norm2_g,
        'ev_w_in': ev_w_in, 's5_log_dt': s5_log_dt, 's5_lambda_re': s5_lambda_re,
        's5_lambda_im': s5_lambda_im, 's5_b_re': s5_b_re, 's5_b_im': s5_b_im, 's5_c_re': s5_c_re,
        's5_c_im': s5_c_im, 's5_d': s5_d, 's5_glu_w': s5_glu_w, 's5_glu_b': s5_glu_b,
        'ev_w_out': ev_w_out, 'od_w_in': od_w_in, 'conf_dw': conf_dw, 'conf_dw_b': conf_dw_b,
        'conf_ln_g': conf_ln_g, 'conf_ln_b': conf_ln_b, 'gdn_conv': gdn_conv, 'gdn_a_log': gdn_a_log,
        'gdn_dt_bias': gdn_dt_bias, 'gdn_norm_g': gdn_norm_g, 'od_w_out': od_w_out,
        'ffn_w_gate': ffn_w_gate, 'ffn_w_up': ffn_w_up, 'ffn_w_down': ffn_w_down,
        'moe_router': moe_router, 'moe_w_gate': moe_w_gate, 'moe_w_up': moe_w_up,
        'moe_w_down': moe_w_down, 'final_norm_g': final_norm_g,
    }


def reference(x, c, ada_w, ada_b, norm1_g, norm2_g, ev_w_in, s5_log_dt, s5_lambda_re, s5_lambda_im,
              s5_b_re, s5_b_im, s5_c_re, s5_c_im, s5_d, s5_glu_w, s5_glu_b, ev_w_out, od_w_in,
              conf_dw, conf_dw_b, conf_ln_g, conf_ln_b, gdn_conv, gdn_a_log, gdn_dt_bias, gdn_norm_g,
              od_w_out, ffn_w_gate, ffn_w_up, ffn_w_down, moe_router, moe_w_gate, moe_w_up,
              moe_w_down, final_norm_g):
    cond = jax.nn.silu(c)
    for layer in range(DEPTH):
        i = layer // 2
        mod = (cond @ ada_w[layer] + ada_b[layer])[:, None, :]
        shift1, scale1, gate1, shift2, scale2, gate2 = jnp.split(mod, 6, axis=-1)

        h = rmsnorm(x, norm1_g[layer]) * (1.0 + scale1) + shift1
        if layer % 2 == 0:
            p = h @ ev_w_in[i]
            y_a = s5_mixer(p[..., :S5_WIDTH], s5_log_dt[i], s5_lambda_re[i], s5_lambda_im[i],
                           s5_b_re[i], s5_b_im[i], s5_c_re[i], s5_c_im[i], s5_d[i],
                           s5_glu_w[i], s5_glu_b[i])
            y_b = dilated_attention(p[..., S5_WIDTH:])
            y = jnp.concatenate([y_a, y_b], axis=-1) @ ev_w_out[i]
        else:
            p = h @ od_w_in[i]
            y_c = conformer_conv(p[..., :2 * CONV_WIDTH], conf_dw[i], conf_dw_b[i],
                                 conf_ln_g[i], conf_ln_b[i])
            y_d = gated_deltanet(p[..., 2 * CONV_WIDTH:], gdn_conv[i], gdn_a_log[i],
                                 gdn_dt_bias[i], gdn_norm_g[i])
            y = jnp.concatenate([y_c, y_d], axis=-1) @ od_w_out[i]
        x = x + gate1 * y

        h = rmsnorm(x, norm2_g[layer]) * (1.0 + scale2) + shift2
        if layer % 2 == 0:
            f = swiglu(h, ffn_w_gate[i], ffn_w_up[i], ffn_w_down[i])
        else:
            f = moe_ffn(h, moe_router[i], moe_w_gate[i], moe_w_up[i], moe_w_down[i])
        x = x + gate2 * f
    return rmsnorm(x, final_norm_g)
```

```python
import functools
import math

import jax
import jax.numpy as jnp
from jax import lax
from jax.experimental import pallas as pl
from jax.experimental.pallas import tpu as pltpu

F32 = jnp.float32
BF16 = jnp.bfloat16
HIGHEST = lax.Precision.HIGHEST

EPS = 1e-6
NEG_INF = -1e30
LANES = 128
SUBLANES = 8
VMEM_LIMIT_BYTES = 56 * 1024 * 1024

S5_WIDTH = 1024
S5_GROUP = 16
S5_STATE = 64
S5_CLUSTER = LANES // S5_GROUP
DIL_PATTERNS = ((128, 1), (512, 4), (2048, 16))
DIL_HEADS = 4
DIL_HEAD_DIM = 128
DIL_WIDTH = DIL_HEADS * DIL_HEAD_DIM
ATT_QBLK = 128
ATT_SUPER = 2048
CONV_WIDTH = 1024
CONV_KERNEL = 31
CONV_HALO = 32
GDN_HEADS = 8
GDN_HEAD_DIM = 128
GDN_WIDTH = GDN_HEADS * GDN_HEAD_DIM
GDN_CONV = 4
GDN_CHUNK = 64
GDN_PAIR = 2 * GDN_CHUNK
GDN_HALO = 8
N_EXPERTS = 8
TOP_K = 2


def _cparams(*sem):
    return pltpu.CompilerParams(dimension_semantics=sem, vmem_limit_bytes=VMEM_LIMIT_BYTES)


def _sigmoid(v):
    return 1.0 / (1.0 + jnp.exp(-v))


def _silu(v):
    return v * _sigmoid(v)


def _norm_mod(x, g, scale, shift):
    ms = jnp.mean(x * x, axis=-1, keepdims=True)
    return x * lax.rsqrt(ms + EPS) * (g * (1.0 + scale)) + shift


def _dot(a, b):
    return jnp.dot(a, b, preferred_element_type=F32)


def _dot_f32(a, b):
    return jnp.dot(a, b, preferred_element_type=F32, precision=HIGHEST)


def _dot_nt(a, b):
    return lax.dot_general(a, b, (((1,), (1,)), ((), ())), preferred_element_type=F32)


def _dot_tn(a, b):
    return lax.dot_general(a, b, (((0,), (0,)), ((), ())), preferred_element_type=F32)


def _adaln_kernel(c_ref, w_ref, b_ref, o_ref):
    c = c_ref[...]
    cond = _silu(c).astype(BF16)
    o_ref[0] = _dot(cond, w_ref[0].astype(BF16)) + b_ref[0]


def adaln(c, ada_w, ada_b, tn=1024):
    depth, d, n_out = ada_w.shape
    b = c.shape[0]
    c_pad = jnp.zeros((SUBLANES, d), F32).at[:b].set(c)
    out = pl.pallas_call(
        _adaln_kernel,
        grid=(depth, n_out // tn),
        in_specs=[pl.BlockSpec((SUBLANES, d), lambda l, j: (0, 0)),
                  pl.BlockSpec((1, d, tn), lambda l, j: (l, 0, j)),
                  pl.BlockSpec((1, 1, tn), lambda l, j: (l, 0, j))],
        out_specs=pl.BlockSpec((1, SUBLANES, tn), lambda l, j: (l, 0, j)),
        out_shape=jax.ShapeDtypeStruct((depth, SUBLANES, n_out), F32),
        compiler_params=_cparams("parallel", "parallel"),
        name="adaln",
    )(c_pad, ada_w, ada_b.reshape(depth, 1, n_out))
    return out[:, :b]


def _inproj_kernel(x_ref, g_ref, sc_ref, sh_ref, w_ref, o_ref, h_ref):
    @pl.when(pl.program_id(1) == 0)
    def _():
        h_ref[...] = _norm_mod(x_ref[...], g_ref[...], sc_ref[0], sh_ref[0]).astype(BF16)

    o_ref[...] = _dot(h_ref[...], w_ref[...]).astype(o_ref.dtype)


def inproj(x2, g, scale, shift, w, seq, tm, tn, out_dtype=F32):
    n, d = x2.shape
    n_out = w.shape[1]
    return pl.pallas_call(
        _inproj_kernel,
        grid=(n // tm, n_out // tn),
        in_specs=[pl.BlockSpec((tm, d), lambda i, j: (i, 0)),
                  pl.BlockSpec((1, d), lambda i, j: (0, 0)),
                  pl.BlockSpec((1, 1, d), lambda i, j: (i * tm // seq, 0, 0)),
                  pl.BlockSpec((1, 1, d), lambda i, j: (i * tm // seq, 0, 0)),
                  pl.BlockSpec((d, tn), lambda i, j: (0, j))],
        out_specs=pl.BlockSpec((tm, tn), lambda i, j: (i, j)),
        out_shape=jax.ShapeDtypeStruct((n, n_out), out_dtype),
        scratch_shapes=[pltpu.VMEM((tm, d), BF16)],
        compiler_params=_cparams("parallel", "arbitrary"),
        name="inproj",
    )(x2, g.reshape(1, d), scale, shift, w)


def _outproj_kernel(ya_ref, yb_ref, wa_ref, wb_ref, x_ref, gate_ref, o_ref):
    acc = _dot(ya_ref[...], wa_ref[...]) + _dot(yb_ref[...], wb_ref[...])
    o_ref[...] = x_ref[...] + gate_ref[0] * acc


def outproj(ya, yb, wa, wb, x2, gate, seq, tm=512):
    n, d = x2.shape
    ka, kb = ya.shape[1], yb.shape[1]
    return pl.pallas_call(
        _outproj_kernel,
        grid=(n // tm,),
        in_specs=[pl.BlockSpec((tm, ka), lambda i: (i, 0)),
                  pl.BlockSpec((tm, kb), lambda i: (i, 0)),
                  pl.BlockSpec((ka, d), lambda i: (0, 0)),
                  pl.BlockSpec((kb, d), lambda i: (0, 0)),
                  pl.BlockSpec((tm, d), lambda i: (i, 0)),
                  pl.BlockSpec((1, 1, d), lambda i: (i * tm // seq, 0, 0))],
        out_specs=pl.BlockSpec((tm, d), lambda i: (i, 0)),
        out_shape=jax.ShapeDtypeStruct((n, d), F32),
        compiler_params=_cparams("parallel"),
        name="outproj",
    )(ya, yb, wa, wb, x2, gate)


def _ffn_kernel(x_ref, g_ref, sc_ref, sh_ref, gate_ref, wg_ref, wu_ref, wd_ref, o_ref, h_ref, acc_ref):
    j = pl.program_id(1)

    @pl.when(j == 0)
    def _():
        h_ref[...] = _norm_mod(x_ref[...], g_ref[...], sc_ref[0], sh_ref[0]).astype(BF16)
        acc_ref[...] = jnp.zeros_like(acc_ref)

    h = h_ref[...]
    act = (_silu(_dot(h, wg_ref[...])) * _dot(h, wu_ref[...])).astype(BF16)
    acc_ref[...] += _dot(act, wd_ref[...])

    @pl.when(j == pl.num_programs(1) - 1)
    def _():
        o_ref[...] = x_ref[...] + gate_ref[0] * acc_ref[...]


def ffn(x2, g, scale, shift, gate, wg, wu, wd, seq, tm=512, tf=512):
    n, d = x2.shape
    dff = wg.shape[1]
    mod_spec = pl.BlockSpec((1, 1, d), lambda i, j: (i * tm // seq, 0, 0))
    return pl.pallas_call(
        _ffn_kernel,
        grid=(n // tm, dff // tf),
        in_specs=[pl.BlockSpec((tm, d), lambda i, j: (i, 0)),
                  pl.BlockSpec((1, d), lambda i, j: (0, 0)),
                  mod_spec, mod_spec, mod_spec,
                  pl.BlockSpec((d, tf), lambda i, j: (0, j)),
                  pl.BlockSpec((d, tf), lambda i, j: (0, j)),
                  pl.BlockSpec((tf, d), lambda i, j: (j, 0))],
        out_specs=pl.BlockSpec((tm, d), lambda i, j: (i, 0)),
        out_shape=jax.ShapeDtypeStruct((n, d), F32),
        scratch_shapes=[pltpu.VMEM((tm, d), BF16), pltpu.VMEM((tm, d), F32)],
        compiler_params=_cparams("parallel", "arbitrary"),
        name="ffn",
    )(x2, g.reshape(1, d), scale, shift, gate, wg, wu, wd)


def _s5_constants(log_dt, lam_re, lam_im, b_re, b_im, c_re, c_im):
    groups, state = lam_re.shape
    ncl = groups // S5_CLUSTER
    dt = jnp.exp(log_dt.astype(F32))[:, None]
    lr, li = lam_re.astype(F32), lam_im.astype(F32)

    def a_pow(k):
        mag = jnp.exp(lr * dt * k)
        return mag * jnp.cos(li * dt * k), mag * jnp.sin(li * dt * k)

    ab_re, ab_im = a_pow(1.0)
    nr, ni = ab_re - 1.0, ab_im
    den = lr * lr + li * li
    zr = (nr * lr + ni * li) / den
    zi = (ni * lr - nr * li) / den
    br, bi = b_re.astype(F32), b_im.astype(F32)
    bbar_re = zr[..., None] * br - zi[..., None] * bi
    bbar_im = zr[..., None] * bi + zi[..., None] * br
    eye = jnp.eye(S5_CLUSTER, dtype=F32)

    def in_blocks(bb):
        bb = bb.reshape(ncl, S5_CLUSTER, state, S5_GROUP)
        return jnp.einsum('kgph,gq->kghqp', bb, eye).reshape(ncl, LANES, S5_CLUSTER * state)

    def out_blocks(cc):
        cc = cc.reshape(ncl, S5_CLUSTER, S5_GROUP, state)
        return jnp.einsum('kghp,gq->kgpqh', cc, eye).reshape(ncl, S5_CLUSTER * state, LANES)

    bcat = jnp.concatenate([in_blocks(bbar_re), in_blocks(bbar_im)], axis=-1).astype(BF16)
    ccat = jnp.concatenate([out_blocks(c_re.astype(F32)), -out_blocks(c_im.astype(F32))], axis=1).astype(BF16)

    rows = jnp.arange(SUBLANES, dtype=F32)[:, None]
    consts = []
    for k in (1, 2, 4):
        pr, pi = a_pow(float(k))
        keep = rows >= k
        consts += [jnp.where(keep, pr.reshape(1, -1), 0.0), jnp.where(keep, pi.reshape(1, -1), 0.0)]
    mag = jnp.exp((lr * dt).reshape(1, -1) * (rows + 1.0))
    ang = (li * dt).reshape(1, -1) * (rows + 1.0)
    consts += [mag * jnp.cos(ang), mag * jnp.sin(ang)]
    return bcat, ccat, jnp.concatenate(consts, axis=0)


def _gelu_tanh(y):
    return 0.5 * y * (1.0 + jnp.tanh(math.sqrt(2.0 / math.pi) * (y + 0.044715 * (y * y * y))))


def _s5_kernel(u_ref, bcat_ref, ccat_ref, sc_ref, d_ref, gw_ref, gb_ref, o_ref, xs_ref, carry_ref, y_ref):
    ncl = bcat_ref.shape[0]
    half = bcat_ref.shape[2] // 2
    tc = u_ref.shape[1]

    @pl.when(pl.program_id(1) == 0)
    def _():
        carry_ref[...] = jnp.zeros_like(carry_ref)

    u = u_ref[0]
    ub = u.astype(BF16)
    for kc in range(ncl):
        xs_ref[:, kc * 2 * half:(kc + 1) * 2 * half] = _dot(ub[:, kc * LANES:(kc + 1) * LANES], bcat_ref[kc])

    def block(r, _):
        row = pl.multiple_of(r * SUBLANES, SUBLANES)
        for kc in range(ncl):
            for q in range(half // LANES):
                cre = kc * 2 * half + q * LANES
                cim = cre + half
                cc = kc * half + q * LANES
                xr = xs_ref[pl.ds(row, SUBLANES), cre:cre + LANES]
                xi = xs_ref[pl.ds(row, SUBLANES), cim:cim + LANES]
                for m, k in enumerate((1, 2, 4)):
                    ar = sc_ref[16 * m:16 * m + 8, cc:cc + LANES]
                    ai = sc_ref[16 * m + 8:16 * m + 16, cc:cc + LANES]
                    sr = pltpu.roll(xr, k, 0)
                    si = pltpu.roll(xi, k, 0)
                    xr, xi = xr + ar * sr - ai * si, xi + ar * si + ai * sr
                pr = sc_ref[48:56, cc:cc + LANES]
                pi = sc_ref[56:64, cc:cc + LANES]
                cr = jnp.broadcast_to(carry_ref[0:1, cc:cc + LANES], (SUBLANES, LANES))
                ci = jnp.broadcast_to(carry_ref[1:2, cc:cc + LANES], (SUBLANES, LANES))
                xr, xi = xr + pr * cr - pi * ci, xi + pr * ci + pi * cr
                xs_ref[pl.ds(row, SUBLANES), cre:cre + LANES] = xr
                xs_ref[pl.ds(row, SUBLANES), cim:cim + LANES] = xi
                carry_ref[0:1, cc:cc + LANES] = xr[SUBLANES - 1:SUBLANES]
                carry_ref[1:2, cc:cc + LANES] = xi[SUBLANES - 1:SUBLANES]
        return 0

    lax.fori_loop(0, tc // SUBLANES, block, 0)

    for kc in range(ncl):
        xb = xs_ref[:, kc * 2 * half:(kc + 1) * 2 * half].astype(BF16)
        y_ref[:, kc * LANES:(kc + 1) * LANES] = _dot(xb, ccat_ref[kc])
    y = _gelu_tanh(y_ref[...] + d_ref[...] * u)
    z = _dot(y.astype(BF16), gw_ref[...]) + gb_ref[...]
    o_ref[0] = (y * _sigmoid(z)).astype(o_ref.dtype)


def s5_mixer(p3, consts, d_skip, glu_w, glu_b, tc=256):
    bsz, seq, _ = p3.shape
    bcat, ccat, sc = consts
    ncl, _, two_half = bcat.shape
    width = ncl * LANES
    full = lambda *shape: pl.BlockSpec(shape, lambda b, c: (0,) * len(shape))
    return pl.pallas_call(
        _s5_kernel,
        grid=(bsz, seq // tc),
        in_specs=[pl.BlockSpec((1, tc, width), lambda b, c: (b, c, 0)),
                  full(*bcat.shape), full(*ccat.shape), full(*sc.shape),
                  full(1, width), full(width, width), full(1, width)],
        out_specs=pl.BlockSpec((1, tc, width), lambda b, c: (b, c, 0)),
        out_shape=jax.ShapeDtypeStruct((bsz, seq, width), BF16),
        scratch_shapes=[pltpu.VMEM((tc, ncl * two_half), F32),
                        pltpu.VMEM((2, ncl * two_half // 2), F32),
                        pltpu.VMEM((tc, width), F32)],
        compiler_params=_cparams("parallel", "arbitrary"),
        name="s5",
    )(p3, bcat, ccat, sc, d_skip.reshape(1, width).astype(F32), glu_w.astype(BF16),
      glu_b.reshape(1, width).astype(F32))


def _attn_kernel(*refs):
    ng = len(DIL_PATTERNS)
    in_refs = refs[:5 * ng]
    o_ref = refs[5 * ng]
    qd, kx, vx, od, lsed, og, lg = refs[5 * ng + 1:]
    s = pl.program_id(2)
    sb, qb = ATT_SUPER, ATT_QBLK
    scale = DIL_HEAD_DIM ** -0.5

    ji = lax.broadcasted_iota(jnp.int32, (qb, 2 * qb), 0)
    ii = lax.broadcasted_iota(jnp.int32, (qb, 2 * qb), 1)
    band = (ii >= ji) & (ii <= ji + qb)

    for g, (_, d) in enumerate(DIL_PATTERNS):
        q_ref, kc_ref, vc_ref, kp_ref, vp_ref = in_refs[5 * g:5 * g + 5]
        n_r = sb // d
        nbr = n_r // qb
        ext = n_r + qb
        for r in range(d):
            qd[r * n_r:(r + 1) * n_r, :] = (q_ref[0, pl.ds(r, n_r, stride=d), :] * scale).astype(BF16)
            kx[r * ext:r * ext + qb, :] = kp_ref[0, pl.ds(r, qb, stride=d), :].astype(BF16)
            kx[r * ext + qb:(r + 1) * ext, :] = kc_ref[0, pl.ds(r, n_r, stride=d), :].astype(BF16)
            vx[r * ext:r * ext + qb, :] = vp_ref[0, pl.ds(r, qb, stride=d), :].astype(BF16)
            vx[r * ext + qb:(r + 1) * ext, :] = vc_ref[0, pl.ds(r, n_r, stride=d), :].astype(BF16)

        def block(idx, _, nbr=nbr, ext=ext):
            r = idx // nbr
            nb = idx % nbr
            q0 = pl.multiple_of(idx * qb, qb)
            k0 = pl.multiple_of(r * ext + nb * qb, qb)
            q = qd[pl.ds(q0, qb), :]
            kw = kx[pl.ds(k0, 2 * qb), :]
            vw = vx[pl.ds(k0, 2 * qb), :]
            sc = _dot_nt(q, kw)
            first_key = jnp.where(jnp.logical_or(nb > 0, s > 0), 0, qb)
            valid = band & (ii >= first_key)
            sc = jnp.where(valid, sc, NEG_INF)
            m = jnp.max(sc, axis=-1, keepdims=True)
            p = jnp.exp(sc - m)
            l = jnp.sum(p, axis=-1, keepdims=True)
            o = _dot(p.astype(BF16), vw) * (1.0 / l)
            od[pl.ds(q0, qb), :] = o
            lsed[pl.ds(q0, qb), :] = jnp.broadcast_to(m + jnp.log(l), (qb, LANES))
            return 0

        lax.fori_loop(0, sb // qb, block, 0)

        for r in range(d):
            og[g, pl.ds(r, n_r, stride=d), :] = od[r * n_r:(r + 1) * n_r, :]
            lg[g, pl.ds(r, n_r, stride=d), :] = lsed[r * n_r:(r + 1) * n_r, :]

    mx = jnp.maximum(jnp.maximum(lg[0], lg[1]), lg[2])
    num = jnp.zeros((sb, LANES), F32)
    den = jnp.zeros((sb, LANES), F32)
    for g in range(ng):
        w = jnp.exp(lg[g] - mx)
        num = num + w * og[g]
        den = den + w
    o_ref[0] = (num / den).astype(o_ref.dtype)


def dilated_attention(p3, col0):
    bsz, seq, _ = p3.shape
    sb = ATT_SUPER
    cb0 = col0 // LANES
    in_specs, args = [], []
    for g, (_, d) in enumerate(DIL_PATTERNS):
        base = cb0 + g * 3 * DIL_HEADS
        prev_rows = ATT_QBLK * d
        per_super = sb // prev_rows

        def cur(which, base=base):
            return pl.BlockSpec((1, sb, LANES), lambda b, h, s: (b, s, base + which * DIL_HEADS + h))

        def prev(which, base=base, per_super=per_super, prev_rows=prev_rows):
            return pl.BlockSpec((1, prev_rows, LANES),
                                lambda b, h, s: (b, jnp.maximum(s * per_super - 1, 0), base + which * DIL_HEADS + h))

        in_specs += [cur(0), cur(1), cur(2), prev(1), prev(2)]
        args += [p3] * 5
    max_ext = max(sb + ATT_QBLK * d for _, d in DIL_PATTERNS)
    return pl.pallas_call(
        _attn_kernel,
        grid=(bsz, DIL_HEADS, seq // sb),
        in_specs=in_specs,
        out_specs=pl.BlockSpec((1, sb, LANES), lambda b, h, s: (b, s, h)),
        out_shape=jax.ShapeDtypeStruct((bsz, seq, DIL_WIDTH), BF16),
        scratch_shapes=[pltpu.VMEM((sb, LANES), BF16),
                        pltpu.VMEM((max_ext, LANES), BF16),
                        pltpu.VMEM((max_ext, LANES), BF16),
                        pltpu.VMEM((sb, LANES), F32),
                        pltpu.VMEM((sb, LANES), F32),
                        pltpu.VMEM((len(DIL_PATTERNS), sb, LANES), F32),
                        pltpu.VMEM((len(DIL_PATTERNS), sb, LANES), F32)],
        compiler_params=_cparams("parallel", "parallel", "arbitrary"),
        name="dilated_attention",
    )(*args)


def _conformer_kernel(a_ref, gt_ref, dw_ref, dwb_ref, lng_ref, lnb_ref, o_ref, ypad_ref, cv_ref):
    tc = a_ref.shape[1]
    width = a_ref.shape[2]
    halo = CONV_HALO
    rows = 64

    @pl.when(pl.program_id(1) == 0)
    def _():
        ypad_ref[0:halo, :] = jnp.zeros((halo, width), F32)

    @pl.when(pl.program_id(1) > 0)
    def _():
        ypad_ref[0:halo, :] = ypad_ref[tc:tc + halo, :]

    ypad_ref[halo:halo + tc, :] = a_ref[0] * _sigmoid(gt_ref[0])

    first = halo - (CONV_KERNEL - 1)
    for strip in range(width // LANES):
        c0 = strip * LANES
        for rt in range(tc // rows):
            acc = jnp.zeros((rows, LANES), F32)
            for j in range(CONV_KERNEL):
                r0 = first + j + rt * rows
                acc = acc + dw_ref[j:j + 1, c0:c0 + LANES] * ypad_ref[r0:r0 + rows, c0:c0 + LANES]
            cv_ref[rt * rows:(rt + 1) * rows, c0:c0 + LANES] = acc + dwb_ref[:, c0:c0 + LANES]

    y = cv_ref[...]
    mu = jnp.mean(y, axis=-1, keepdims=True)
    yc = y - mu
    var = jnp.mean(yc * yc, axis=-1, keepdims=True)
    y = yc * lax.rsqrt(var + EPS) * lng_ref[...] + lnb_ref[...]
    o_ref[0] = _silu(y).astype(o_ref.dtype)


def conformer_conv(p3, dw, dw_b, ln_g, ln_b, tc=256):
    bsz, seq, _ = p3.shape
    width = CONV_WIDTH
    kpad = CONV_HALO
    dw_pad = jnp.zeros((kpad, width), F32).at[:CONV_KERNEL].set(dw.astype(F32))
    row = lambda v: v.reshape(1, width).astype(F32)
    full = lambda *shape: pl.BlockSpec(shape, lambda b, c: (0,) * len(shape))
    return pl.pallas_call(
        _conformer_kernel,
        grid=(bsz, seq // tc),
        in_specs=[pl.BlockSpec((1, tc, width), lambda b, c: (b, c, 0)),
                  pl.BlockSpec((1, tc, width), lambda b, c: (b, c, 1)),
                  full(kpad, width), full(1, width), full(1, width), full(1, width)],
        out_specs=pl.BlockSpec((1, tc, width), lambda b, c: (b, c, 0)),
        out_shape=jax.ShapeDtypeStruct((bsz, seq, width), BF16),
        scratch_shapes=[pltpu.VMEM((tc + CONV_HALO, width), F32), pltpu.VMEM((tc, width), F32)],
        compiler_params=_cparams("parallel", "arbitrary"),
        name="conformer_conv",
    )(p3, p3, dw_pad, row(dw_b), row(ln_g), row(ln_b))


def _gdn_prep_kernel(q_ref, k_ref, v_ref, cw_ref, pg_ref, alog_ref, dtb_ref,
                     qo_ref, ko_ref, vo_ref, go_ref, pad_ref):
    tc = q_ref.shape[1]
    width = q_ref.shape[2]
    halo = GDN_HALO
    first = halo - (GDN_CONV - 1)
    for t, (src, dst) in enumerate(((q_ref, qo_ref), (k_ref, ko_ref), (v_ref, vo_ref))):
        @pl.when(pl.program_id(1) == 0)
        def _():
            pad_ref[t, 0:halo, :] = jnp.zeros((halo, width), F32)

        @pl.when(pl.program_id(1) > 0)
        def _():
            pad_ref[t, 0:halo, :] = pad_ref[t, tc:tc + halo, :]

        pad_ref[t, halo:halo + tc, :] = src[0]
        for h in range(width // LANES):
            c0 = h * LANES
            acc = jnp.zeros((tc, LANES), F32)
            for j in range(GDN_CONV):
                acc = acc + (cw_ref[j:j + 1, t * width + c0:t * width + c0 + LANES]
                             * pad_ref[t, first + j:first + j + tc, c0:c0 + LANES])
            xh = _silu(acc)
            if t < 2:
                xh = xh * lax.rsqrt(jnp.sum(xh * xh, axis=-1, keepdims=True) + EPS)
                if t == 0:
                    xh = xh * (GDN_HEAD_DIM ** -0.5)
            dst[0, :, c0:c0 + LANES] = xh

    pg = pg_ref[...]
    lane = lax.broadcasted_iota(jnp.int32, pg.shape, 1)
    beta = _sigmoid(pg)
    sp = pg + dtb_ref[...]
    softplus = jnp.maximum(sp, 0.0) + jnp.log(1.0 + jnp.exp(-jnp.abs(sp)))
    g = -jnp.exp(alog_ref[...]) * softplus
    ri = lax.broadcasted_iota(jnp.int32, (tc, tc), 0)
    ci = lax.broadcasted_iota(jnp.int32, (tc, tc), 1)
    tri = ((ri >= ci) & ((ri // GDN_CHUNK) == (ci // GDN_CHUNK))).astype(F32)
    gc = _dot_f32(tri, g)
    go_ref[...] = jnp.where(lane < GDN_HEADS, beta, gc)


def gdn_prep(p3, pg, conv_w, a_log, dt_bias, col0, tc=256):
    bsz, seq, _ = p3.shape
    width = GDN_WIDTH
    cb = col0 // width
    cw = jnp.zeros((SUBLANES, 3 * width), F32).at[:GDN_CONV].set(conv_w.astype(F32))
    lanes_pad = lambda v: jnp.zeros((1, LANES), F32).at[0, GDN_HEADS:2 * GDN_HEADS].set(v.astype(F32))
    blk = lambda k: pl.BlockSpec((1, tc, width), lambda b, c: (b, c, k))
    full = lambda *shape: pl.BlockSpec(shape, lambda b, c: (0,) * len(shape))
    nchunks = seq // tc
    sds = jax.ShapeDtypeStruct((bsz, seq, width), F32)
    return pl.pallas_call(
        _gdn_prep_kernel,
        grid=(bsz, nchunks),
        in_specs=[blk(cb), blk(cb + 1), blk(cb + 2), full(SUBLANES, 3 * width),
                  pl.BlockSpec((tc, LANES), lambda b, c: (b * nchunks + c, 0)),
                  full(1, LANES), full(1, LANES)],
        out_specs=[blk(0), blk(0), blk(0), pl.BlockSpec((tc, LANES), lambda b, c: (b * nchunks + c, 0))],
        out_shape=[sds, sds, sds, jax.ShapeDtypeStruct((bsz * seq, LANES), F32)],
        scratch_shapes=[pltpu.VMEM((3, tc + GDN_HALO, width), F32)],
        compiler_params=_cparams("parallel", "arbitrary"),
        name="gdn_prep",
    )(p3, p3, p3, cw, pg, lanes_pad(a_log), lanes_pad(dt_bias))


def _unit_lower_inverse(nm, eye):
    x = eye - nm
    p = _dot_f32(nm, nm)
    steps = int(math.log2(GDN_CHUNK)) - 1
    for it in range(steps):
        x = x + _dot_f32(x, p)
        if it < steps - 1:
            p = _dot_f32(p, p)
    return x


def _gdn_kernel(q_ref, k_ref, v_ref, z_ref, gcol_ref, grow_ref, ng_ref, o_ref, s_ref):
    tc = q_ref.shape[1]
    pr = GDN_PAIR
    cs = GDN_CHUNK

    @pl.when(pl.program_id(1) == 0)
    def _():
        s_ref[...] = jnp.zeros_like(s_ref)

    ri = lax.broadcasted_iota(jnp.int32, (pr, pr), 0)
    ci = lax.broadcasted_iota(jnp.int32, (pr, pr), 1)
    same = (ri // cs) == (ci // cs)
    causal = same & (ri >= ci)
    strict = same & (ri > ci)
    eye = (ri == ci).astype(F32)
    ng = ng_ref[...]

    for n in range(tc // pr):
        rows = slice(n * pr, (n + 1) * pr)
        for h in range(GDN_HEADS):
            cols = slice(h * LANES, (h + 1) * LANES)
            q = q_ref[0, rows, cols]
            k = k_ref[0, rows, cols]
            v = v_ref[0, rows, cols]
            beta = gcol_ref[0, rows, h:h + 1]
            gc = gcol_ref[0, rows, GDN_HEADS + h:GDN_HEADS + h + 1]
            gr = grow_ref[0, n, h:h + 1, :]
            decay = jnp.exp(jnp.minimum(gc - gr, 0.0))
            kb = k * beta
            eg = jnp.exp(gc)
            kb16 = kb.astype(BF16)
            k16 = k.astype(BF16)
            nm = jnp.where(strict, _dot_nt(kb16, k16) * decay, 0.0)
            tinv = _unit_lower_inverse(nm, eye)
            rhs = jnp.concatenate([v * beta, kb * eg], axis=-1)
            sol = _dot_f32(tinv, rhs)
            u = sol[:, :LANES]
            w16 = sol[:, LANES:].astype(BF16)
            attn = jnp.where(causal, _dot_nt(q.astype(BF16), k16) * decay, 0.0).astype(BF16)
            qg16 = (q * eg).astype(BF16)

            st = s_ref[h]
            v_new, o_state = [], []
            for cidx in range(pr // cs):
                cr = slice(cidx * cs, (cidx + 1) * cs)
                st16 = st.astype(BF16)
                vn = u[cr] - _dot(w16[cr], st16)
                o_state.append(_dot(qg16[cr], st16))
                g_last = gc[cidx * cs + cs - 1:cidx * cs + cs, :]
                kdec = (k[cr] * jnp.exp(g_last - gc[cr])).astype(BF16)
                st = st * jnp.exp(g_last) + _dot_tn(kdec, vn.astype(BF16))
                v_new.append(vn)
            s_ref[h] = st
            o = jnp.concatenate(o_state, axis=0) + _dot(attn, jnp.concatenate(v_new, axis=0).astype(BF16))
            on = o * lax.rsqrt(jnp.mean(o * o, axis=-1, keepdims=True) + EPS) * ng
            o_ref[0, rows, cols] = (on * _silu(z_ref[0, rows, cols])).astype(o_ref.dtype)


def gdn_core(q, k, v, p3, z_col0, gates, norm_g, tc=128):
    bsz, seq, width = q.shape
    npair = tc // GDN_PAIR
    gcol = gates.reshape(bsz, seq, LANES)
    grow = gcol[:, :, GDN_HEADS:2 * GDN_HEADS].reshape(bsz, seq // GDN_PAIR, GDN_PAIR, GDN_HEADS)
    grow = jnp.swapaxes(grow, 2, 3)
    blk = lambda kk: pl.BlockSpec((1, tc, width), lambda b, c: (b, c, kk))
    return pl.pallas_call(
        _gdn_kernel,
        grid=(bsz, seq // tc),
        in_specs=[blk(0), blk(0), blk(0), blk(z_col0 // width),
                  pl.BlockSpec((1, tc, LANES), lambda b, c: (b, c, 0)),
                  pl.BlockSpec((1, npair, GDN_HEADS, GDN_PAIR), lambda b, c: (b, c, 0, 0)),
                  pl.BlockSpec((1, LANES), lambda b, c: (0, 0))],
        out_specs=blk(0),
        out_shape=jax.ShapeDtypeStruct((bsz, seq, width), BF16),
        scratch_shapes=[pltpu.VMEM((GDN_HEADS, GDN_HEAD_DIM, GDN_HEAD_DIM), F32)],
        compiler_params=_cparams("parallel", "arbitrary"),
        name="gdn_core",
    )(q, k, v, p3, gcol, grow, norm_g.reshape(1, LANES).astype(F32))


def _router_kernel(x_ref, g_ref, sc_ref, sh_ref, rw_ref, h_ref, rt_ref):
    h = _norm_mod(x_ref[...], g_ref[...], sc_ref[0], sh_ref[0])
    h_ref[...] = h
    logits = _dot_f32(h, rw_ref[...])
    lane = lax.broadcasted_iota(jnp.int32, logits.shape, 1).astype(F32)
    lg = jnp.where(lane < N_EXPERTS, logits, -jnp.inf)
    m1 = jnp.max(lg, axis=-1, keepdims=True)
    i1 = jnp.min(jnp.where(lg == m1, lane, float(LANES)), axis=-1, keepdims=True)
    lg2 = jnp.where(lane == i1, -jnp.inf, lg)
    m2 = jnp.max(lg2, axis=-1, keepdims=True)
    i2 = jnp.min(jnp.where(lg2 == m2, lane, float(LANES)), axis=-1, keepdims=True)
    e2 = jnp.exp(m2 - m1)
    w1 = 1.0 / (1.0 + e2)
    w2 = e2 * w1
    rt = jnp.where(lane == 0.0, i1,
                   jnp.where(lane == 1.0, i2,
                             jnp.where(lane == 2.0, w1, jnp.where(lane == 3.0, w2, 0.0))))
    rt_ref[...] = rt


def router(x2, g, scale, shift, rw, seq, tm=256):
    n, d = x2.shape
    rw_pad = jnp.zeros((d, LANES), F32).at[:, :N_EXPERTS].set(rw.astype(F32))
    mod_spec = pl.BlockSpec((1, 1, d), lambda i: (i * tm // seq, 0, 0))
    return pl.pallas_call(
        _router_kernel,
        grid=(n // tm,),
        in_specs=[pl.BlockSpec((tm, d), lambda i: (i, 0)),
                  pl.BlockSpec((1, d), lambda i: (0, 0)),
                  mod_spec, mod_spec,
                  pl.BlockSpec((d, LANES), lambda i: (0, 0))],
        out_specs=[pl.BlockSpec((tm, d), lambda i: (i, 0)), pl.BlockSpec((tm, LANES), lambda i: (i, 0))],
        out_shape=[jax.ShapeDtypeStruct((n, d), F32), jax.ShapeDtypeStruct((n, LANES), F32)],
        compiler_params=_cparams("parallel"),
        name="router",
    )(x2, g.reshape(1, d), scale, shift, rw_pad)


def _row_copy(src_hbm, dst, sem, src_row, dst_row):
    return pltpu.make_async_copy(src_hbm.at[pl.ds(src_row, 1)], dst.at[pl.ds(dst_row, 1)], sem)


def _dispatch_kernel(idx_ref, src_hbm, o_ref, buf, sem):
    rows = o_ref.shape[0]

    def start(r, _):
        _row_copy(src_hbm, buf, sem, idx_ref[0, 0, r], r).start()
        return 0

    def wait(r, _):
        _row_copy(src_hbm, buf, sem, idx_ref[0, 0, r], r).wait()
        return 0

    lax.fori_loop(0, rows, start, 0)
    lax.fori_loop(0, rows, wait, 0)
    o_ref[...] = buf[...].astype(o_ref.dtype)


def dispatch(h2, src_tok, tile):
    n, d = h2.shape
    slots = src_tok.shape[0]
    nt = slots // tile
    return pl.pallas_call(
        _dispatch_kernel,
        grid=(nt,),
        in_specs=[pl.BlockSpec((1, 1, tile), lambda i: (i, 0, 0), memory_space=pltpu.SMEM),
                  pl.BlockSpec(memory_space=pl.ANY)],
        out_specs=pl.BlockSpec((tile, d), lambda i: (i, 0)),
        out_shape=jax.ShapeDtypeStruct((slots, d), BF16),
        scratch_shapes=[pltpu.VMEM((tile, d), F32), pltpu.SemaphoreType.DMA(())],
        compiler_params=_cparams("arbitrary"),
        name="moe_dispatch",
    )(src_tok.reshape(nt, 1, tile), h2)


def _moe_ffn_kernel(te_ref, nu_ref, xs_ref, rw_ref, wg_ref, wu_ref, wd_ref, o_ref, acc_ref):
    i = pl.program_id(0)
    j = pl.program_id(1)
    last = pl.num_programs(1) - 1
    used = i < nu_ref[0]

    @pl.when(used)
    def _():
        @pl.when(j == 0)
        def _():
            acc_ref[...] = jnp.zeros_like(acc_ref)

        xs = xs_ref[...]
        act = (_silu(_dot(xs, wg_ref[0])) * _dot(xs, wu_ref[0])).astype(BF16)
        acc_ref[...] += _dot(act, wd_ref[0])

        @pl.when(j == last)
        def _():
            o_ref[...] = acc_ref[...] * rw_ref[...]

    @pl.when(jnp.logical_not(used) & (j == last))
    def _():
        o_ref[...] = jnp.zeros_like(o_ref)


def moe_ffn(xs, row_w, tile_expert, n_used, wg, wu, wd, tile, tf=256):
    slots, d = xs.shape
    dff = wg.shape[2]
    nj = dff // tf
    jj = lambda i, j, te, nu: jnp.where(i < nu[0], j, nj - 1)
    grid_spec = pltpu.PrefetchScalarGridSpec(
        num_scalar_prefetch=2,
        grid=(slots // tile, nj),
        in_specs=[pl.BlockSpec((tile, d), lambda i, j, te, nu: (i, 0)),
                  pl.BlockSpec((tile, 1), lambda i, j, te, nu: (i, 0)),
                  pl.BlockSpec((1, d, tf), lambda i, j, te, nu: (te[i], 0, jj(i, j, te, nu))),
                  pl.BlockSpec((1, d, tf), lambda i, j, te, nu: (te[i], 0, jj(i, j, te, nu))),
                  pl.BlockSpec((1, tf, d), lambda i, j, te, nu: (te[i], jj(i, j, te, nu), 0))],
        out_specs=pl.BlockSpec((tile, d), lambda i, j, te, nu: (i, 0)),
        scratch_shapes=[pltpu.VMEM((tile, d), F32)])
    return pl.pallas_call(
        _moe_ffn_kernel,
        grid_spec=grid_spec,
        out_shape=jax.ShapeDtypeStruct((slots, d), F32),
        compiler_params=_cparams("arbitrary", "arbitrary"),
        name="moe_ffn",
    )(tile_expert, n_used, xs, row_w, wg, wu, wd)


def _combine_kernel(p0_ref, p1_ref, ys_hbm, x_ref, gate_ref, o_ref, buf0, buf1, sem0, sem1):
    rows = o_ref.shape[0]

    def start(r, _):
        _row_copy(ys_hbm, buf0, sem0, p0_ref[0, 0, r], r).start()
        _row_copy(ys_hbm, buf1, sem1, p1_ref[0, 0, r], r).start()
        return 0

    def wait(r, _):
        _row_copy(ys_hbm, buf0, sem0, p0_ref[0, 0, r], r).wait()
        _row_copy(ys_hbm, buf1, sem1, p1_ref[0, 0, r], r).wait()
        return 0

    lax.fori_loop(0, rows, start, 0)
    lax.fori_loop(0, rows, wait, 0)
    o_ref[...] = x_ref[...] + gate_ref[0] * (buf0[...] + buf1[...])


def combine(ys, pos0, pos1, x2, gate, seq, tm=256):
    n, d = x2.shape
    nt = n // tm
    idx_spec = pl.BlockSpec((1, 1, tm), lambda i: (i, 0, 0), memory_space=pltpu.SMEM)
    return pl.pallas_call(
        _combine_kernel,
        grid=(nt,),
        in_specs=[idx_spec, idx_spec, pl.BlockSpec(memory_space=pl.ANY),
                  pl.BlockSpec((tm, d), lambda i: (i, 0)),
                  pl.BlockSpec((1, 1, d), lambda i: (i * tm // seq, 0, 0))],
        out_specs=pl.BlockSpec((tm, d), lambda i: (i, 0)),
        out_shape=jax.ShapeDtypeStruct((n, d), F32),
        scratch_shapes=[pltpu.VMEM((tm, d), F32), pltpu.VMEM((tm, d), F32),
                        pltpu.SemaphoreType.DMA(()), pltpu.SemaphoreType.DMA(())],
        compiler_params=_cparams("arbitrary"),
        name="moe_combine",
    )(pos0.reshape(nt, 1, tm), pos1.reshape(nt, 1, tm), ys, x2, gate)


def _dispatch_plan(rt, tile):
    n = rt.shape[0]
    slots = n * TOP_K + N_EXPERTS * tile
    nt = slots // tile
    e_flat = rt[:, :TOP_K].astype(jnp.int32).reshape(-1)
    w_flat = rt[:, TOP_K:2 * TOP_K].reshape(-1)
    onehot = (e_flat[:, None] == jnp.arange(N_EXPERTS, dtype=jnp.int32)[None, :]).astype(jnp.int32)
    ranks = jnp.cumsum(onehot, axis=0) - onehot
    rank = jnp.sum(ranks * onehot, axis=1)
    counts = jnp.sum(onehot, axis=0)
    padded = ((counts + tile - 1) // tile) * tile
    ends = jnp.cumsum(padded)
    offs = ends - padded
    pos = offs[e_flat] + rank
    tok = jnp.arange(n * TOP_K, dtype=jnp.int32) // TOP_K
    src_tok = jnp.zeros((slots,), jnp.int32).at[pos].set(tok)
    row_w = jnp.zeros((slots,), F32).at[pos].set(w_flat)
    n_used = (ends[-1] // tile).astype(jnp.int32).reshape(1)
    tile_start = jnp.arange(nt, dtype=jnp.int32) * tile
    tile_expert = jnp.sum((tile_start[:, None] >= ends[None, :]).astype(jnp.int32), axis=1)
    last_expert = jnp.sum((jnp.maximum(ends[-1] - tile, 0) >= ends).astype(jnp.int32))
    tile_expert = jnp.where(tile_start < ends[-1], tile_expert, last_expert).astype(jnp.int32)
    pos2 = pos.reshape(n, TOP_K)
    return src_tok, row_w.reshape(slots, 1), tile_expert, n_used, pos2[:, 0], pos2[:, 1]


def _final_norm_kernel(x_ref, g_ref, o_ref):
    x = x_ref[...]
    o_ref[...] = x * lax.rsqrt(jnp.mean(x * x, axis=-1, keepdims=True) + EPS) * g_ref[...]


def final_norm(x2, g, tm=512):
    n, d = x2.shape
    return pl.pallas_call(
        _final_norm_kernel,
        grid=(n // tm,),
        in_specs=[pl.BlockSpec((tm, d), lambda i: (i, 0)), pl.BlockSpec((1, d), lambda i: (0, 0))],
        out_specs=pl.BlockSpec((tm, d), lambda i: (i, 0)),
        out_shape=jax.ShapeDtypeStruct((n, d), F32),
        compiler_params=_cparams("parallel"),
        name="final_norm",
    )(x2, g.reshape(1, d).astype(F32))


MOE_TILE = 512


def kernel(x, c, ada_w, ada_b, norm1_g, norm2_g, ev_w_in, s5_log_dt, s5_lambda_re, s5_lambda_im, s5_b_re, s5_b_im, s5_c_re, s5_c_im, s5_d, s5_glu_w, s5_glu_b, ev_w_out, od_w_in, conf_dw, conf_dw_b, conf_ln_g, conf_ln_b, gdn_conv, gdn_a_log, gdn_dt_bias, gdn_norm_g, od_w_out, ffn_w_gate, ffn_w_up, ffn_w_down, moe_router, moe_w_gate, moe_w_up, moe_w_down, final_norm_g):
    bsz, seq, d = x.shape
    n = bsz * seq
    depth = ada_w.shape[0]
    x2 = x.reshape(n, d).astype(F32)
    mod = adaln(c.astype(F32), ada_w, ada_b)

    for layer in range(depth):
        i = layer // 2
        m = mod[layer]
        shift1, scale1, gate1, shift2, scale2, gate2 = (m[:, k * d:(k + 1) * d][:, None, :] for k in range(6))
        g1 = norm1_g[layer].astype(F32)
        g2 = norm2_g[layer].astype(F32)
        if layer % 2 == 0:
            p = inproj(x2, g1, scale1, shift1, ev_w_in[i].astype(BF16), seq, tm=512, tn=512)
            p3 = p.reshape(bsz, seq, -1)
            consts = _s5_constants(s5_log_dt[i], s5_lambda_re[i], s5_lambda_im[i], s5_b_re[i], s5_b_im[i],
                                   s5_c_re[i], s5_c_im[i])
            y_a = s5_mixer(p3, consts, s5_d[i], s5_glu_w[i], s5_glu_b[i])
            y_b = dilated_attention(p3, S5_WIDTH)
            w_out = ev_w_out[i].astype(BF16)
            x2 = outproj(y_a.reshape(n, -1), y_b.reshape(n, -1), w_out[:S5_WIDTH], w_out[S5_WIDTH:],
                         x2, gate1, seq)
            x2 = ffn(x2, g2, scale2, shift2, gate2, ffn_w_gate[i].astype(BF16), ffn_w_up[i].astype(BF16),
                     ffn_w_down[i].astype(BF16), seq)
        else:
            w_in = od_w_in[i]
            main = 2 * CONV_WIDTH + 4 * GDN_WIDTH
            p = inproj(x2, g1, scale1, shift1, w_in[:, :main].astype(BF16), seq, tm=512, tn=512)
            w_gates = jnp.zeros((d, LANES), F32).at[:, :2 * GDN_HEADS].set(w_in[:, main:]).astype(BF16)
            pg = inproj(x2, g1, scale1, shift1, w_gates, seq, tm=512, tn=LANES)
            p3 = p.reshape(bsz, seq, -1)
            y_c = conformer_conv(p3, conf_dw[i], conf_dw_b[i], conf_ln_g[i], conf_ln_b[i])
            q, k, v, gates = gdn_prep(p3, pg, gdn_conv[i], gdn_a_log[i], gdn_dt_bias[i], 2 * CONV_WIDTH)
            y_d = gdn_core(q, k, v, p3, 2 * CONV_WIDTH + 3 * GDN_WIDTH, gates, gdn_norm_g[i])
            w_out = od_w_out[i].astype(BF16)
            x2 = outproj(y_c.reshape(n, -1), y_d.reshape(n, -1), w_out[:CONV_WIDTH], w_out[CONV_WIDTH:],
                         x2, gate1, seq)
            h2, rt = router(x2, g2, scale2, shift2, moe_router[i], seq)
            src_tok, row_w, tile_expert, n_used, pos0, pos1 = _dispatch_plan(rt, MOE_TILE)
            xs = dispatch(h2, src_tok, MOE_TILE)
            ys = moe_ffn(xs, row_w, tile_expert, n_used, moe_w_gate[i].astype(BF16),
                         moe_w_up[i].astype(BF16), moe_w_down[i].astype(BF16), MOE_TILE)
            x2 = combine(ys, pos0, pos1, x2, gate2, seq)
    return final_norm(x2, final_norm_g).reshape(bsz, seq, d)
```

```python
import functools
import math

import jax
import jax.numpy as jnp
from jax import lax
from jax.experimental import pallas as pl
from jax.experimental.pallas import tpu as pltpu

F32 = jnp.float32
BF16 = jnp.bfloat16
HIGHEST = lax.Precision.HIGHEST

EPS = 1e-6
NEG_INF = -1e30
LANES = 128
SUBLANES = 8
VMEM_LIMIT_BYTES = 56 * 1024 * 1024

S5_WIDTH = 1024
S5_GROUP = 16
S5_STATE = 64
S5_CLUSTER = LANES // S5_GROUP
DIL_PATTERNS = ((128, 1), (512, 4), (2048, 16))
DIL_HEADS = 4
DIL_HEAD_DIM = 128
DIL_WIDTH = DIL_HEADS * DIL_HEAD_DIM
ATT_QBLK = 128
ATT_SUPER = 2048
ATT_INTERLEAVE = 4
CONV_WIDTH = 1024
CONV_KERNEL = 31
CONV_HALO = 32
GDN_HEADS = 8
GDN_HEAD_DIM = 128
GDN_WIDTH = GDN_HEADS * GDN_HEAD_DIM
GDN_CONV = 4
GDN_CHUNK = 64
GDN_PAIR = 2 * GDN_CHUNK
GDN_HALO = 8
N_EXPERTS = 8
TOP_K = 2


def _cparams(*sem):
    return pltpu.CompilerParams(dimension_semantics=sem, vmem_limit_bytes=VMEM_LIMIT_BYTES)


def _sigmoid(v):
    return 1.0 / (1.0 + jnp.exp(-v))


def _silu(v):
    return v * _sigmoid(v)


def _norm_mod(x, g, scale, shift):
    ms = jnp.mean(x * x, axis=-1, keepdims=True)
    return x * lax.rsqrt(ms + EPS) * (g * (1.0 + scale)) + shift


def _dot(a, b):
    return jnp.dot(a, b, preferred_element_type=F32)


def _dot_f32(a, b):
    return jnp.dot(a, b, preferred_element_type=F32, precision=HIGHEST)


def _dot_nt(a, b):
    return lax.dot_general(a, b, (((1,), (1,)), ((), ())), preferred_element_type=F32)


def _dot_tn(a, b):
    return lax.dot_general(a, b, (((0,), (0,)), ((), ())), preferred_element_type=F32)


def _adaln_kernel(c_ref, w_ref, b_ref, o_ref):
    c = c_ref[...]
    cond = _silu(c).astype(BF16)
    o_ref[0] = _dot(cond, w_ref[0].astype(BF16)) + b_ref[0]


def adaln(c, ada_w, ada_b, tn=1024):
    depth, d, n_out = ada_w.shape
    b = c.shape[0]
    c_pad = jnp.zeros((SUBLANES, d), F32).at[:b].set(c)
    out = pl.pallas_call(
        _adaln_kernel,
        grid=(depth, n_out // tn),
        in_specs=[pl.BlockSpec((SUBLANES, d), lambda l, j: (0, 0)),
                  pl.BlockSpec((1, d, tn), lambda l, j: (l, 0, j)),
                  pl.BlockSpec((1, 1, tn), lambda l, j: (l, 0, j))],
        out_specs=pl.BlockSpec((1, SUBLANES, tn), lambda l, j: (l, 0, j)),
        out_shape=jax.ShapeDtypeStruct((depth, SUBLANES, n_out), F32),
        compiler_params=_cparams("parallel", "parallel"),
        name="adaln",
    )(c_pad, ada_w, ada_b.reshape(depth, 1, n_out))
    return out[:, :b]


def _inproj_kernel(x_ref, g_ref, sc_ref, sh_ref, w_ref, o_ref, h_ref):
    @pl.when(pl.program_id(1) == 0)
    def _():
        h_ref[...] = _norm_mod(x_ref[...], g_ref[...], sc_ref[0], sh_ref[0]).astype(BF16)

    o_ref[...] = _dot(h_ref[...], w_ref[...]).astype(o_ref.dtype)


def inproj(x2, g, scale, shift, w, seq, tm, tn, out_dtype=F32):
    n, d = x2.shape
    n_out = w.shape[1]
    return pl.pallas_call(
        _inproj_kernel,
        grid=(n // tm, n_out // tn),
        in_specs=[pl.BlockSpec((tm, d), lambda i, j: (i, 0)),
                  pl.BlockSpec((1, d), lambda i, j: (0, 0)),
                  pl.BlockSpec((1, 1, d), lambda i, j: (i * tm // seq, 0, 0)),
                  pl.BlockSpec((1, 1, d), lambda i, j: (i * tm // seq, 0, 0)),
                  pl.BlockSpec((d, tn), lambda i, j: (0, j))],
        out_specs=pl.BlockSpec((tm, tn), lambda i, j: (i, j)),
        out_shape=jax.ShapeDtypeStruct((n, n_out), out_dtype),
        scratch_shapes=[pltpu.VMEM((tm, d), BF16)],
        compiler_params=_cparams("parallel", "arbitrary"),
        name="inproj",
    )(x2, g.reshape(1, d), scale, shift, w)


def _outproj_kernel(ya_ref, yb_ref, wa_ref, wb_ref, x_ref, gate_ref, o_ref):
    acc = _dot(ya_ref[...], wa_ref[...]) + _dot(yb_ref[...], wb_ref[...])
    o_ref[...] = x_ref[...] + gate_ref[0] * acc


def outproj(ya, yb, wa, wb, x2, gate, seq, tm=512):
    n, d = x2.shape
    ka, kb = ya.shape[1], yb.shape[1]
    return pl.pallas_call(
        _outproj_kernel,
        grid=(n // tm,),
        in_specs=[pl.BlockSpec((tm, ka), lambda i: (i, 0)),
                  pl.BlockSpec((tm, kb), lambda i: (i, 0)),
                  pl.BlockSpec((ka, d), lambda i: (0, 0)),
                  pl.BlockSpec((kb, d), lambda i: (0, 0)),
                  pl.BlockSpec((tm, d), lambda i: (i, 0)),
                  pl.BlockSpec((1, 1, d), lambda i: (i * tm // seq, 0, 0))],
        out_specs=pl.BlockSpec((tm, d), lambda i: (i, 0)),
        out_shape=jax.ShapeDtypeStruct((n, d), F32),
        compiler_params=_cparams("parallel"),
        name="outproj",
    )(ya, yb, wa, wb, x2, gate)


def _ffn_up_kernel(x_ref, g_ref, sc_ref, sh_ref, wg_ref, wu_ref, o_ref, h_ref):
    @pl.when(pl.program_id(1) == 0)
    def _():
        h_ref[...] = _norm_mod(x_ref[...], g_ref[...], sc_ref[0], sh_ref[0]).astype(BF16)

    h = h_ref[...]
    o_ref[...] = (_silu(_dot(h, wg_ref[...])) * _dot(h, wu_ref[...])).astype(o_ref.dtype)


def _ffn_down_kernel(a_ref, wd_ref, x_ref, gate_ref, o_ref):
    o_ref[...] = x_ref[...] + gate_ref[0] * _dot(a_ref[...], wd_ref[...])


def ffn(x2, g, scale, shift, gate, wg, wu, wd, seq, tm=1024, tf=512, tn=512):
    n, d = x2.shape
    dff = wg.shape[1]
    mod_spec = pl.BlockSpec((1, 1, d), lambda i, j: (i * tm // seq, 0, 0))
    act = pl.pallas_call(
        _ffn_up_kernel,
        grid=(n // tm, dff // tf),
        in_specs=[pl.BlockSpec((tm, d), lambda i, j: (i, 0)),
                  pl.BlockSpec((1, d), lambda i, j: (0, 0)),
                  mod_spec, mod_spec,
                  pl.BlockSpec((d, tf), lambda i, j: (0, j)),
                  pl.BlockSpec((d, tf), lambda i, j: (0, j))],
        out_specs=pl.BlockSpec((tm, tf), lambda i, j: (i, j)),
        out_shape=jax.ShapeDtypeStruct((n, dff), BF16),
        scratch_shapes=[pltpu.VMEM((tm, d), BF16)],
        compiler_params=_cparams("parallel", "arbitrary"),
        name="ffn_up",
    )(x2, g.reshape(1, d), scale, shift, wg, wu)
    return pl.pallas_call(
        _ffn_down_kernel,
        grid=(n // tm, d // tn),
        in_specs=[pl.BlockSpec((tm, dff), lambda i, j: (i, 0)),
                  pl.BlockSpec((dff, tn), lambda i, j: (0, j)),
                  pl.BlockSpec((tm, tn), lambda i, j: (i, j)),
                  pl.BlockSpec((1, 1, tn), lambda i, j: (i * tm // seq, 0, j))],
        out_specs=pl.BlockSpec((tm, tn), lambda i, j: (i, j)),
        out_shape=jax.ShapeDtypeStruct((n, d), F32),
        compiler_params=_cparams("parallel", "arbitrary"),
        name="ffn_down",
    )(act, wd, x2, gate)


def _s5_constants(log_dt, lam_re, lam_im, b_re, b_im, c_re, c_im):
    groups, state = lam_re.shape
    ncl = groups // S5_CLUSTER
    dt = jnp.exp(log_dt.astype(F32))[:, None]
    lr, li = lam_re.astype(F32), lam_im.astype(F32)

    def a_pow(k):
        mag = jnp.exp(lr * dt * k)
        return mag * jnp.cos(li * dt * k), mag * jnp.sin(li * dt * k)

    ab_re, ab_im = a_pow(1.0)
    nr, ni = ab_re - 1.0, ab_im
    den = lr * lr + li * li
    zr = (nr * lr + ni * li) / den
    zi = (ni * lr - nr * li) / den
    br, bi = b_re.astype(F32), b_im.astype(F32)
    bbar_re = zr[..., None] * br - zi[..., None] * bi
    bbar_im = zr[..., None] * bi + zi[..., None] * br
    eye = jnp.eye(S5_CLUSTER, dtype=F32)

    def in_blocks(bb):
        bb = bb.reshape(ncl, S5_CLUSTER, state, S5_GROUP)
        return jnp.einsum('kgph,gq->kghqp', bb, eye).reshape(ncl, LANES, S5_CLUSTER * state)

    def out_blocks(cc):
        cc = cc.reshape(ncl, S5_CLUSTER, S5_GROUP, state)
        return jnp.einsum('kghp,gq->kgpqh', cc, eye).reshape(ncl, S5_CLUSTER * state, LANES)

    bcat = jnp.concatenate([in_blocks(bbar_re), in_blocks(bbar_im)], axis=-1).astype(BF16)
    ccat = jnp.concatenate([out_blocks(c_re.astype(F32)), -out_blocks(c_im.astype(F32))], axis=1).astype(BF16)

    rows = jnp.arange(SUBLANES, dtype=F32)[:, None]
    consts = []
    for k in (1, 2, 4):
        pr, pi = a_pow(float(k))
        keep = rows >= k
        consts += [jnp.where(keep, pr.reshape(1, -1), 0.0), jnp.where(keep, pi.reshape(1, -1), 0.0)]
    mag = jnp.exp((lr * dt).reshape(1, -1) * (rows + 1.0))
    ang = (li * dt).reshape(1, -1) * (rows + 1.0)
    consts += [mag * jnp.cos(ang), mag * jnp.sin(ang)]
    return bcat, ccat, jnp.concatenate(consts, axis=0)


def _gelu_tanh(y):
    return 0.5 * y * (1.0 + jnp.tanh(math.sqrt(2.0 / math.pi) * (y + 0.044715 * (y * y * y))))


def _s5_kernel(u_ref, bcat_ref, ccat_ref, sc_ref, d_ref, gw_ref, gb_ref, o_ref, xs_ref, carry_ref, y_ref):
    ncl = bcat_ref.shape[0]
    half = bcat_ref.shape[2] // 2
    tc = u_ref.shape[1]

    @pl.when(pl.program_id(1) == 0)
    def _():
        carry_ref[...] = jnp.zeros_like(carry_ref)

    u = u_ref[0]
    ub = u.astype(BF16)
    for kc in range(ncl):
        xs_ref[:, kc * 2 * half:(kc + 1) * 2 * half] = _dot(ub[:, kc * LANES:(kc + 1) * LANES], bcat_ref[kc])

    def block(r, _):
        row = pl.multiple_of(r * SUBLANES, SUBLANES)
        for kc in range(ncl):
            for q in range(half // LANES):
                cre = kc * 2 * half + q * LANES
                cim = cre + half
                cc = kc * half + q * LANES
                xr = xs_ref[pl.ds(row, SUBLANES), cre:cre + LANES]
                xi = xs_ref[pl.ds(row, SUBLANES), cim:cim + LANES]
                for m, k in enumerate((1, 2, 4)):
                    ar = sc_ref[16 * m:16 * m + 8, cc:cc + LANES]
                    ai = sc_ref[16 * m + 8:16 * m + 16, cc:cc + LANES]
                    sr = pltpu.roll(xr, k, 0)
                    si = pltpu.roll(xi, k, 0)
                    xr, xi = xr + ar * sr - ai * si, xi + ar * si + ai * sr
                pr = sc_ref[48:56, cc:cc + LANES]
                pi = sc_ref[56:64, cc:cc + LANES]
                cr = jnp.broadcast_to(carry_ref[0:1, cc:cc + LANES], (SUBLANES, LANES))
                ci = jnp.broadcast_to(carry_ref[1:2, cc:cc + LANES], (SUBLANES, LANES))
                xr, xi = xr + pr * cr - pi * ci, xi + pr * ci + pi * cr
                xs_ref[pl.ds(row, SUBLANES), cre:cre + LANES] = xr
                xs_ref[pl.ds(row, SUBLANES), cim:cim + LANES] = xi
                carry_ref[0:1, cc:cc + LANES] = xr[SUBLANES - 1:SUBLANES]
                carry_ref[1:2, cc:cc + LANES] = xi[SUBLANES - 1:SUBLANES]
        return 0

    lax.fori_loop(0, tc // SUBLANES, block, 0)

    for kc in range(ncl):
        xb = xs_ref[:, kc * 2 * half:(kc + 1) * 2 * half].astype(BF16)
        y_ref[:, kc * LANES:(kc + 1) * LANES] = _dot(xb, ccat_ref[kc])
    y = _gelu_tanh(y_ref[...] + d_ref[...] * u)
    z = _dot(y.astype(BF16), gw_ref[...]) + gb_ref[...]
    o_ref[0] = (y * _sigmoid(z)).astype(o_ref.dtype)


def s5_mixer(p3, consts, d_skip, glu_w, glu_b, tc=256):
    bsz, seq, _ = p3.shape
    bcat, ccat, sc = consts
    ncl, _, two_half = bcat.shape
    width = ncl * LANES
    full = lambda *shape: pl.BlockSpec(shape, lambda b, c: (0,) * len(shape))
    return pl.pallas_call(
        _s5_kernel,
        grid=(bsz, seq // tc),
        in_specs=[pl.BlockSpec((1, tc, width), lambda b, c: (b, c, 0)),
                  full(*bcat.shape), full(*ccat.shape), full(*sc.shape),
                  full(1, width), full(width, width), full(1, width)],
        out_specs=pl.BlockSpec((1, tc, width), lambda b, c: (b, c, 0)),
        out_shape=jax.ShapeDtypeStruct((bsz, seq, width), BF16),
        scratch_shapes=[pltpu.VMEM((tc, ncl * two_half), F32),
                        pltpu.VMEM((2, ncl * two_half // 2), F32),
                        pltpu.VMEM((tc, width), F32)],
        compiler_params=_cparams("parallel", "arbitrary"),
        name="s5",
    )(p3, bcat, ccat, sc, d_skip.reshape(1, width).astype(F32), glu_w.astype(BF16),
      glu_b.reshape(1, width).astype(F32))


def _attn_kernel(*refs):
    ng = len(DIL_PATTERNS)
    in_refs = refs[:5 * ng]
    o_ref = refs[5 * ng]
    qd, kx, vx, od, lsed, og, lg = refs[5 * ng + 1:]
    s = pl.program_id(2)
    sb, qb = ATT_SUPER, ATT_QBLK
    scale = DIL_HEAD_DIM ** -0.5

    ji = lax.broadcasted_iota(jnp.int32, (qb, 2 * qb), 0)
    ii = lax.broadcasted_iota(jnp.int32, (qb, 2 * qb), 1)
    band = (ii >= ji) & (ii <= ji + qb)

    for g, (_, d) in enumerate(DIL_PATTERNS):
        q_ref, kc_ref, vc_ref, kp_ref, vp_ref = in_refs[5 * g:5 * g + 5]
        n_r = sb // d
        nbr = n_r // qb
        ext = n_r + qb
        for r in range(d):
            qd[r * n_r:(r + 1) * n_r, :] = (q_ref[0, pl.ds(r, n_r, stride=d), :] * scale).astype(BF16)
            kx[r * ext:r * ext + qb, :] = kp_ref[0, pl.ds(r, qb, stride=d), :].astype(BF16)
            kx[r * ext + qb:(r + 1) * ext, :] = kc_ref[0, pl.ds(r, n_r, stride=d), :].astype(BF16)
            vx[r * ext:r * ext + qb, :] = vp_ref[0, pl.ds(r, qb, stride=d), :].astype(BF16)
            vx[r * ext + qb:(r + 1) * ext, :] = vc_ref[0, pl.ds(r, n_r, stride=d), :].astype(BF16)

        def blocks(it, _, nbr=nbr, ext=ext):
            ts = range(ATT_INTERLEAVE)
            idx = [it * ATT_INTERLEAVE + t for t in ts]
            q0 = [pl.multiple_of(idx[t] * qb, qb) for t in ts]
            k0 = [pl.multiple_of((idx[t] // nbr) * ext + (idx[t] % nbr) * qb, qb) for t in ts]
            sc = [_dot_nt(qd[pl.ds(q0[t], qb), :], kx[pl.ds(k0[t], 2 * qb), :]) for t in ts]
            first_key = [jnp.where(jnp.logical_or(idx[t] % nbr > 0, s > 0), 0, qb) for t in ts]
            sc = [jnp.where(band & (ii >= first_key[t]), sc[t], NEG_INF) for t in ts]
            m = [jnp.max(sc[t], axis=-1, keepdims=True) for t in ts]
            p = [jnp.exp(sc[t] - m[t]) for t in ts]
            l = [jnp.sum(p[t], axis=-1, keepdims=True) for t in ts]
            o = [_dot(p[t].astype(BF16), vx[pl.ds(k0[t], 2 * qb), :]) * (1.0 / l[t]) for t in ts]
            for t in ts:
                od[pl.ds(q0[t], qb), :] = o[t]
                lsed[pl.ds(q0[t], qb), :] = jnp.broadcast_to(m[t] + jnp.log(l[t]), (qb, LANES))
            return 0

        lax.fori_loop(0, sb // qb // ATT_INTERLEAVE, blocks, 0)

        for r in range(d):
            og[g, pl.ds(r, n_r, stride=d), :] = od[r * n_r:(r + 1) * n_r, :]
            lg[g, pl.ds(r, n_r, stride=d), :] = lsed[r * n_r:(r + 1) * n_r, :]

    mx = jnp.maximum(jnp.maximum(lg[0], lg[1]), lg[2])
    num = jnp.zeros((sb, LANES), F32)
    den = jnp.zeros((sb, LANES), F32)
    for g in range(ng):
        w = jnp.exp(lg[g] - mx)
        num = num + w * og[g]
        den = den + w
    o_ref[0] = (num / den).astype(o_ref.dtype)


def dilated_attention(p3, col0):
    bsz, seq, _ = p3.shape
    sb = ATT_SUPER
    cb0 = col0 // LANES
    in_specs, args = [], []
    for g, (_, d) in enumerate(DIL_PATTERNS):
        base = cb0 + g * 3 * DIL_HEADS
        prev_rows = ATT_QBLK * d
        per_super = sb // prev_rows

        def cur(which, base=base):
            return pl.BlockSpec((1, sb, LANES), lambda b, h, s: (b, s, base + which * DIL_HEADS + h))

        def prev(which, base=base, per_super=per_super, prev_rows=prev_rows):
            return pl.BlockSpec((1, prev_rows, LANES),
                                lambda b, h, s: (b, jnp.maximum(s * per_super - 1, 0), base + which * DIL_HEADS + h))

        in_specs += [cur(0), cur(1), cur(2), prev(1), prev(2)]
        args += [p3] * 5
    max_ext = max(sb + ATT_QBLK * d for _, d in DIL_PATTERNS)
    return pl.pallas_call(
        _attn_kernel,
        grid=(bsz, DIL_HEADS, seq // sb),
        in_specs=in_specs,
        out_specs=pl.BlockSpec((1, sb, LANES), lambda b, h, s: (b, s, h)),
        out_shape=jax.ShapeDtypeStruct((bsz, seq, DIL_WIDTH), BF16),
        scratch_shapes=[pltpu.VMEM((sb, LANES), BF16),
                        pltpu.VMEM((max_ext, LANES), BF16),
                        pltpu.VMEM((max_ext, LANES), BF16),
                        pltpu.VMEM((sb, LANES), F32),
                        pltpu.VMEM((sb, LANES), F32),
                        pltpu.VMEM((len(DIL_PATTERNS), sb, LANES), F32),
                        pltpu.VMEM((len(DIL_PATTERNS), sb, LANES), F32)],
        compiler_params=_cparams("parallel", "parallel", "arbitrary"),
        name="dilated_attention",
    )(*args)


def _conformer_kernel(a_ref, gt_ref, dw_ref, dwb_ref, lng_ref, lnb_ref, o_ref, ypad_ref, cv_ref):
    tc = a_ref.shape[1]
    width = a_ref.shape[2]
    halo = CONV_HALO
    rows = 64

    @pl.when(pl.program_id(1) == 0)
    def _():
        ypad_ref[0:halo, :] = jnp.zeros((halo, width), F32)

    @pl.when(pl.program_id(1) > 0)
    def _():
        ypad_ref[0:halo, :] = ypad_ref[tc:tc + halo, :]

    ypad_ref[halo:halo + tc, :] = a_ref[0] * _sigmoid(gt_ref[0])

    first = halo - (CONV_KERNEL - 1)
    for strip in range(width // LANES):
        c0 = strip * LANES
        for rt in range(tc // rows):
            acc = jnp.zeros((rows, LANES), F32)
            for j in range(CONV_KERNEL):
                r0 = first + j + rt * rows
                acc = acc + dw_ref[j:j + 1, c0:c0 + LANES] * ypad_ref[r0:r0 + rows, c0:c0 + LANES]
            cv_ref[rt * rows:(rt + 1) * rows, c0:c0 + LANES] = acc + dwb_ref[:, c0:c0 + LANES]

    y = cv_ref[...]
    mu = jnp.mean(y, axis=-1, keepdims=True)
    yc = y - mu
    var = jnp.mean(yc * yc, axis=-1, keepdims=True)
    y = yc * lax.rsqrt(var + EPS) * lng_ref[...] + lnb_ref[...]
    o_ref[0] = _silu(y).astype(o_ref.dtype)


def conformer_conv(p3, dw, dw_b, ln_g, ln_b, tc=256):
    bsz, seq, _ = p3.shape
    width = CONV_WIDTH
    kpad = CONV_HALO
    dw_pad = jnp.zeros((kpad, width), F32).at[:CONV_KERNEL].set(dw.astype(F32))
    row = lambda v: v.reshape(1, width).astype(F32)
    full = lambda *shape: pl.BlockSpec(shape, lambda b, c: (0,) * len(shape))
    return pl.pallas_call(
        _conformer_kernel,
        grid=(bsz, seq // tc),
        in_specs=[pl.BlockSpec((1, tc, width), lambda b, c: (b, c, 0)),
                  pl.BlockSpec((1, tc, width), lambda b, c: (b, c, 1)),
                  full(kpad, width), full(1, width), full(1, width), full(1, width)],
        out_specs=pl.BlockSpec((1, tc, width), lambda b, c: (b, c, 0)),
        out_shape=jax.ShapeDtypeStruct((bsz, seq, width), BF16),
        scratch_shapes=[pltpu.VMEM((tc + CONV_HALO, width), F32), pltpu.VMEM((tc, width), F32)],
        compiler_params=_cparams("parallel", "arbitrary"),
        name="conformer_conv",
    )(p3, p3, dw_pad, row(dw_b), row(ln_g), row(ln_b))


def _gdn_prep_kernel(q_ref, k_ref, v_ref, cw_ref, pg_ref, alog_ref, dtb_ref,
                     qo_ref, ko_ref, vo_ref, go_ref, pad_ref):
    tc = q_ref.shape[1]
    width = q_ref.shape[2]
    halo = GDN_HALO
    first = halo - (GDN_CONV - 1)
    for t, (src, dst) in enumerate(((q_ref, qo_ref), (k_ref, ko_ref), (v_ref, vo_ref))):
        @pl.when(pl.program_id(1) == 0)
        def _():
            pad_ref[t, 0:halo, :] = jnp.zeros((halo, width), F32)

        @pl.when(pl.program_id(1) > 0)
        def _():
            pad_ref[t, 0:halo, :] = pad_ref[t, tc:tc + halo, :]

        pad_ref[t, halo:halo + tc, :] = src[0]
        for h in range(width // LANES):
            c0 = h * LANES
            acc = jnp.zeros((tc, LANES), F32)
            for j in range(GDN_CONV):
                acc = acc + (cw_ref[j:j + 1, t * width + c0:t * width + c0 + LANES]
                             * pad_ref[t, first + j:first + j + tc, c0:c0 + LANES])
            xh = _silu(acc)
            if t < 2:
                xh = xh * lax.rsqrt(jnp.sum(xh * xh, axis=-1, keepdims=True) + EPS)
                if t == 0:
                    xh = xh * (GDN_HEAD_DIM ** -0.5)
            dst[0, :, c0:c0 + LANES] = xh

    pg = pg_ref[...]
    lane = lax.broadcasted_iota(jnp.int32, pg.shape, 1)
    beta = _sigmoid(pg)
    sp = pg + dtb_ref[...]
    softplus = jnp.maximum(sp, 0.0) + jnp.log(1.0 + jnp.exp(-jnp.abs(sp)))
    g = -jnp.exp(alog_ref[...]) * softplus
    ri = lax.broadcasted_iota(jnp.int32, (tc, tc), 0)
    ci = lax.broadcasted_iota(jnp.int32, (tc, tc), 1)
    tri = ((ri >= ci) & ((ri // GDN_CHUNK) == (ci // GDN_CHUNK))).astype(F32)
    gc = _dot_f32(tri, g)
    go_ref[...] = jnp.where(lane < GDN_HEADS, beta, gc)


def gdn_prep(p3, pg, conv_w, a_log, dt_bias, col0, tc=256):
    bsz, seq, _ = p3.shape
    width = GDN_WIDTH
    cb = col0 // width
    cw = jnp.zeros((SUBLANES, 3 * width), F32).at[:GDN_CONV].set(conv_w.astype(F32))
    lanes_pad = lambda v: jnp.zeros((1, LANES), F32).at[0, GDN_HEADS:2 * GDN_HEADS].set(v.astype(F32))
    blk = lambda k: pl.BlockSpec((1, tc, width), lambda b, c: (b, c, k))
    full = lambda *shape: pl.BlockSpec(shape, lambda b, c: (0,) * len(shape))
    nchunks = seq // tc
    sds = jax.ShapeDtypeStruct((bsz, seq, width), F32)
    return pl.pallas_call(
        _gdn_prep_kernel,
        grid=(bsz, nchunks),
        in_specs=[blk(cb), blk(cb + 1), blk(cb + 2), full(SUBLANES, 3 * width),
                  pl.BlockSpec((tc, LANES), lambda b, c: (b * nchunks + c, 0)),
                  full(1, LANES), full(1, LANES)],
        out_specs=[blk(0), blk(0), blk(0), pl.BlockSpec((tc, LANES), lambda b, c: (b * nchunks + c, 0))],
        out_shape=[sds, sds, sds, jax.ShapeDtypeStruct((bsz * seq, LANES), F32)],
        scratch_shapes=[pltpu.VMEM((3, tc + GDN_HALO, width), F32)],
        compiler_params=_cparams("parallel", "arbitrary"),
        name="gdn_prep",
    )(p3, p3, p3, cw, pg, lanes_pad(a_log), lanes_pad(dt_bias))


def _gdn_kernel(q_ref, k_ref, v_ref, z_ref, gcol_ref, grow_ref, ng_ref, o_ref, s_ref):
    tc = q_ref.shape[1]
    pr = GDN_PAIR
    cs = GDN_CHUNK

    @pl.when(pl.program_id(1) == 0)
    def _():
        s_ref[...] = jnp.zeros_like(s_ref)

    ri = lax.broadcasted_iota(jnp.int32, (pr, pr), 0)
    ci = lax.broadcasted_iota(jnp.int32, (pr, pr), 1)
    same = (ri // cs) == (ci // cs)
    causal = same & (ri >= ci)
    strict = same & (ri > ci)
    eye = (ri == ci).astype(F32)
    ng = ng_ref[...]

    heads = range(GDN_HEADS)
    cols = [slice(h * LANES, (h + 1) * LANES) for h in heads]
    for n in range(tc // pr):
        rows = slice(n * pr, (n + 1) * pr)
        beta = [gcol_ref[0, rows, h:h + 1] for h in heads]
        gc = [gcol_ref[0, rows, GDN_HEADS + h:GDN_HEADS + h + 1] for h in heads]
        eg = [jnp.exp(gc[h]) for h in heads]
        decay = [jnp.exp(jnp.minimum(gc[h] - grow_ref[0, n, h:h + 1, :], 0.0)) for h in heads]
        kb = [k_ref[0, rows, cols[h]] * beta[h] for h in heads]
        k16 = [k_ref[0, rows, cols[h]].astype(BF16) for h in heads]
        nm = [jnp.where(strict, _dot_nt(kb[h].astype(BF16), k16[h]) * decay[h], 0.0) for h in heads]

        x = [eye - nm[h] for h in heads]
        p = nm
        for _ in range(int(math.log2(cs)) - 1):
            p16 = [p[h].astype(BF16) for h in heads]
            p = [_dot(p16[h], p16[h]) for h in heads]
            x = [x[h] + _dot(x[h].astype(BF16), p[h].astype(BF16)) for h in heads]

        sol = [_dot(x[h].astype(BF16),
                    jnp.concatenate([v_ref[0, rows, cols[h]] * beta[h], kb[h] * eg[h]], axis=-1).astype(BF16))
               for h in heads]
        u = [sol[h][:, :LANES] for h in heads]
        w16 = [sol[h][:, LANES:].astype(BF16) for h in heads]
        attn = [jnp.where(causal, _dot_nt(q_ref[0, rows, cols[h]].astype(BF16), k16[h]) * decay[h], 0.0).astype(BF16)
                for h in heads]
        qg16 = [(q_ref[0, rows, cols[h]] * eg[h]).astype(BF16) for h in heads]

        st = [s_ref[h] for h in heads]
        v_new = [[] for _ in heads]
        o_state = [[] for _ in heads]
        for cidx in range(pr // cs):
            cr = slice(cidx * cs, (cidx + 1) * cs)
            st16 = [st[h].astype(BF16) for h in heads]
            vn = [u[h][cr] - _dot(w16[h][cr], st16[h]) for h in heads]
            for h in heads:
                o_state[h].append(_dot(qg16[h][cr], st16[h]))
                v_new[h].append(vn[h])
            g_last = [gc[h][cidx * cs + cs - 1:cidx * cs + cs, :] for h in heads]
            kdec = [(k_ref[0, rows, cols[h]][cr] * jnp.exp(g_last[h] - gc[h][cr])).astype(BF16) for h in heads]
            st = [st[h] * jnp.exp(g_last[h]) + _dot_tn(kdec[h], vn[h].astype(BF16)) for h in heads]
        for h in heads:
            s_ref[h] = st[h]
        for h in heads:
            o = (jnp.concatenate(o_state[h], axis=0)
                 + _dot(attn[h], jnp.concatenate(v_new[h], axis=0).astype(BF16)))
            on = o * lax.rsqrt(jnp.mean(o * o, axis=-1, keepdims=True) + EPS) * ng
            o_ref[0, rows, cols[h]] = (on * _silu(z_ref[0, rows, cols[h]])).astype(o_ref.dtype)


def gdn_core(q, k, v, p3, z_col0, gates, norm_g, tc=128):
    bsz, seq, width = q.shape
    npair = tc // GDN_PAIR
    gcol = gates.reshape(bsz, seq, LANES)
    grow = gcol[:, :, GDN_HEADS:2 * GDN_HEADS].reshape(bsz, seq // GDN_PAIR, GDN_PAIR, GDN_HEADS)
    grow = jnp.swapaxes(grow, 2, 3)
    blk = lambda kk: pl.BlockSpec((1, tc, width), lambda b, c: (b, c, kk))
    return pl.pallas_call(
        _gdn_kernel,
        grid=(bsz, seq // tc),
        in_specs=[blk(0), blk(0), blk(0), blk(z_col0 // width),
                  pl.BlockSpec((1, tc, LANES), lambda b, c: (b, c, 0)),
                  pl.BlockSpec((1, npair, GDN_HEADS, GDN_PAIR), lambda b, c: (b, c, 0, 0)),
                  pl.BlockSpec((1, LANES), lambda b, c: (0, 0))],
        out_specs=blk(0),
        out_shape=jax.ShapeDtypeStruct((bsz, seq, width), BF16),
        scratch_shapes=[pltpu.VMEM((GDN_HEADS, GDN_HEAD_DIM, GDN_HEAD_DIM), F32)],
        compiler_params=_cparams("parallel", "arbitrary"),
        name="gdn_core",
    )(q, k, v, p3, gcol, grow, norm_g.reshape(1, LANES).astype(F32))


def _router_kernel(x_ref, g_ref, sc_ref, sh_ref, rw_ref, h_ref, rt_ref):
    h = _norm_mod(x_ref[...], g_ref[...], sc_ref[0], sh_ref[0])
    h_ref[...] = h
    logits = _dot_f32(h, rw_ref[...])
    lane = lax.broadcasted_iota(jnp.int32, logits.shape, 1).astype(F32)
    lg = jnp.where(lane < N_EXPERTS, logits, -jnp.inf)
    m1 = jnp.max(lg, axis=-1, keepdims=True)
    i1 = jnp.min(jnp.where(lg == m1, lane, float(LANES)), axis=-1, keepdims=True)
    lg2 = jnp.where(lane == i1, -jnp.inf, lg)
    m2 = jnp.max(lg2, axis=-1, keepdims=True)
    i2 = jnp.min(jnp.where(lg2 == m2, lane, float(LANES)), axis=-1, keepdims=True)
    e2 = jnp.exp(m2 - m1)
    w1 = 1.0 / (1.0 + e2)
    w2 = e2 * w1
    rt = jnp.where(lane == 0.0, i1,
                   jnp.where(lane == 1.0, i2,
                             jnp.where(lane == 2.0, w1, jnp.where(lane == 3.0, w2, 0.0))))
    rt_ref[...] = rt


def router(x2, g, scale, shift, rw, seq, tm=256):
    n, d = x2.shape
    rw_pad = jnp.zeros((d, LANES), F32).at[:, :N_EXPERTS].set(rw.astype(F32))
    mod_spec = pl.BlockSpec((1, 1, d), lambda i: (i * tm // seq, 0, 0))
    return pl.pallas_call(
        _router_kernel,
        grid=(n // tm,),
        in_specs=[pl.BlockSpec((tm, d), lambda i: (i, 0)),
                  pl.BlockSpec((1, d), lambda i: (0, 0)),
                  mod_spec, mod_spec,
                  pl.BlockSpec((d, LANES), lambda i: (0, 0))],
        out_specs=[pl.BlockSpec((tm, d), lambda i: (i, 0)), pl.BlockSpec((tm, LANES), lambda i: (i, 0))],
        out_shape=[jax.ShapeDtypeStruct((n, d), F32), jax.ShapeDtypeStruct((n, LANES), F32)],
        compiler_params=_cparams("parallel"),
        name="router",
    )(x2, g.reshape(1, d), scale, shift, rw_pad)


def _row_copy(src_hbm, dst, sem, src_row, dst_row):
    return pltpu.make_async_copy(src_hbm.at[pl.ds(src_row, 1)], dst.at[pl.ds(dst_row, 1)], sem)


GATHER_UNROLL = 8


def _gather_rows(src_hbm, idx_ref, dst, sem):
    def start(r, _):
        _row_copy(src_hbm, dst, sem, idx_ref[0, 0, r], r).start()
        return 0

    lax.fori_loop(0, dst.shape[0], start, 0, unroll=GATHER_UNROLL)


def _wait_rows(src_hbm, idx_ref, dst, sem):
    def wait(r, _):
        _row_copy(src_hbm, dst, sem, idx_ref[0, 0, r], r).wait()
        return 0

    lax.fori_loop(0, dst.shape[0], wait, 0, unroll=GATHER_UNROLL)


def _dispatch_kernel(nu_ref, idx_ref, idx_next_ref, src_hbm, o_ref, buf, sems):
    i = pl.program_id(0)
    n_used = nu_ref[0]

    @pl.when(jnp.logical_and(i == 0, n_used > 0))
    def _():
        _gather_rows(src_hbm, idx_ref, buf.at[0], sems.at[0])

    @pl.when(i + 1 < n_used)
    def _():
        nxt = (i + 1) % 2
        _gather_rows(src_hbm, idx_next_ref, buf.at[nxt], sems.at[nxt])

    @pl.when(i < n_used)
    def _():
        cur = i % 2
        _wait_rows(src_hbm, idx_ref, buf.at[cur], sems.at[cur])
        o_ref[...] = buf[cur].astype(o_ref.dtype)

    @pl.when(i >= n_used)
    def _():
        o_ref[...] = jnp.zeros_like(o_ref)


def dispatch(h2, src_tok, n_used, tile):
    n, d = h2.shape
    slots = src_tok.shape[0]
    nt = slots // tile
    idx = src_tok.reshape(nt, 1, tile)
    return pl.pallas_call(
        _dispatch_kernel,
        grid_spec=pltpu.PrefetchScalarGridSpec(
            num_scalar_prefetch=1,
            grid=(nt,),
            in_specs=[pl.BlockSpec((1, 1, tile), lambda i, nu: (i, 0, 0), memory_space=pltpu.SMEM),
                      pl.BlockSpec((1, 1, tile), lambda i, nu: (jnp.minimum(i + 1, nt - 1), 0, 0),
                                   memory_space=pltpu.SMEM),
                      pl.BlockSpec(memory_space=pl.ANY)],
            out_specs=pl.BlockSpec((tile, d), lambda i, nu: (i, 0)),
            scratch_shapes=[pltpu.VMEM((2, tile, d), F32), pltpu.SemaphoreType.DMA((2,))]),
        out_shape=jax.ShapeDtypeStruct((slots, d), BF16),
        compiler_params=_cparams("arbitrary"),
        name="moe_dispatch",
    )(n_used, idx, idx, h2)


def _moe_up_kernel(te_ref, nu_ref, xs_ref, wg_ref, wu_ref, o_ref):
    used = pl.program_id(0) < nu_ref[0]

    @pl.when(used)
    def _():
        xs = xs_ref[...]
        o_ref[...] = (_silu(_dot(xs, wg_ref[0])) * _dot(xs, wu_ref[0])).astype(o_ref.dtype)

    @pl.when(jnp.logical_not(used))
    def _():
        o_ref[...] = jnp.zeros_like(o_ref)


def _moe_down_kernel(te_ref, nu_ref, a_ref, rw_ref, wd_ref, o_ref):
    used = pl.program_id(0) < nu_ref[0]

    @pl.when(used)
    def _():
        o_ref[...] = _dot(a_ref[...], wd_ref[0]) * rw_ref[...]

    @pl.when(jnp.logical_not(used))
    def _():
        o_ref[...] = jnp.zeros_like(o_ref)


def moe_ffn(xs, row_w, tile_expert, n_used, wg, wu, wd, tile, tf=1408, tn=1024):
    slots, d = xs.shape
    dff = wg.shape[2]

    def jj(nj):
        return lambda i, j, te, nu: jnp.where(i < nu[0], j, nj - 1)

    ju = jj(dff // tf)
    act = pl.pallas_call(
        _moe_up_kernel,
        grid_spec=pltpu.PrefetchScalarGridSpec(
            num_scalar_prefetch=2,
            grid=(slots // tile, dff // tf),
            in_specs=[pl.BlockSpec((tile, d), lambda i, j, te, nu: (i, 0)),
                      pl.BlockSpec((1, d, tf), lambda i, j, te, nu: (te[i], 0, ju(i, j, te, nu))),
                      pl.BlockSpec((1, d, tf), lambda i, j, te, nu: (te[i], 0, ju(i, j, te, nu)))],
            out_specs=pl.BlockSpec((tile, tf), lambda i, j, te, nu: (i, j))),
        out_shape=jax.ShapeDtypeStruct((slots, dff), BF16),
        compiler_params=_cparams("arbitrary", "arbitrary"),
        name="moe_up",
    )(tile_expert, n_used, xs, wg, wu)
    jd = jj(d // tn)
    return pl.pallas_call(
        _moe_down_kernel,
        grid_spec=pltpu.PrefetchScalarGridSpec(
            num_scalar_prefetch=2,
            grid=(slots // tile, d // tn),
            in_specs=[pl.BlockSpec((tile, dff), lambda i, j, te, nu: (i, 0)),
                      pl.BlockSpec((tile, 1), lambda i, j, te, nu: (i, 0)),
                      pl.BlockSpec((1, dff, tn), lambda i, j, te, nu: (te[i], 0, jd(i, j, te, nu)))],
            out_specs=pl.BlockSpec((tile, tn), lambda i, j, te, nu: (i, j))),
        out_shape=jax.ShapeDtypeStruct((slots, d), F32),
        compiler_params=_cparams("arbitrary", "arbitrary"),
        name="moe_down",
    )(tile_expert, n_used, act, row_w, wd)


def _combine_kernel(p0_ref, p1_ref, p0n_ref, p1n_ref, ys_hbm, x_ref, gate_ref, o_ref, buf, sems):
    i = pl.program_id(0)

    @pl.when(i == 0)
    def _():
        _gather_rows(ys_hbm, p0_ref, buf.at[0, 0], sems.at[0, 0])
        _gather_rows(ys_hbm, p1_ref, buf.at[0, 1], sems.at[0, 1])

    @pl.when(i + 1 < pl.num_programs(0))
    def _():
        nxt = (i + 1) % 2
        _gather_rows(ys_hbm, p0n_ref, buf.at[nxt, 0], sems.at[nxt, 0])
        _gather_rows(ys_hbm, p1n_ref, buf.at[nxt, 1], sems.at[nxt, 1])

    cur = i % 2
    _wait_rows(ys_hbm, p0_ref, buf.at[cur, 0], sems.at[cur, 0])
    _wait_rows(ys_hbm, p1_ref, buf.at[cur, 1], sems.at[cur, 1])
    o_ref[...] = x_ref[...] + gate_ref[0] * (buf[cur, 0] + buf[cur, 1])


def combine(ys, pos0, pos1, x2, gate, seq, tm=256):
    n, d = x2.shape
    nt = n // tm
    idx_spec = pl.BlockSpec((1, 1, tm), lambda i: (i, 0, 0), memory_space=pltpu.SMEM)
    nxt_spec = pl.BlockSpec((1, 1, tm), lambda i: (jnp.minimum(i + 1, nt - 1), 0, 0), memory_space=pltpu.SMEM)
    p0 = pos0.reshape(nt, 1, tm)
    p1 = pos1.reshape(nt, 1, tm)
    return pl.pallas_call(
        _combine_kernel,
        grid=(nt,),
        in_specs=[idx_spec, idx_spec, nxt_spec, nxt_spec, pl.BlockSpec(memory_space=pl.ANY),
                  pl.BlockSpec((tm, d), lambda i: (i, 0)),
                  pl.BlockSpec((1, 1, d), lambda i: (i * tm // seq, 0, 0))],
        out_specs=pl.BlockSpec((tm, d), lambda i: (i, 0)),
        out_shape=jax.ShapeDtypeStruct((n, d), F32),
        scratch_shapes=[pltpu.VMEM((2, 2, tm, d), F32), pltpu.SemaphoreType.DMA((2, 2))],
        compiler_params=_cparams("arbitrary"),
        name="moe_combine",
    )(p0, p1, p0, p1, ys, x2, gate)


def _dispatch_plan(rt, tile):
    n = rt.shape[0]
    slots = n * TOP_K + N_EXPERTS * tile
    nt = slots // tile
    e_flat = rt[:, :TOP_K].astype(jnp.int32).reshape(-1)
    w_flat = rt[:, TOP_K:2 * TOP_K].reshape(-1)
    onehot = (e_flat[:, None] == jnp.arange(N_EXPERTS, dtype=jnp.int32)[None, :]).astype(jnp.int32)
    ranks = jnp.cumsum(onehot, axis=0) - onehot
    rank = jnp.sum(ranks * onehot, axis=1)
    counts = jnp.sum(onehot, axis=0)
    padded = ((counts + tile - 1) // tile) * tile
    ends = jnp.cumsum(padded)
    offs = ends - padded
    pos = offs[e_flat] + rank
    tok = jnp.arange(n * TOP_K, dtype=jnp.int32) // TOP_K
    src_tok = jnp.zeros((slots,), jnp.int32).at[pos].set(tok)
    row_w = jnp.zeros((slots,), F32).at[pos].set(w_flat)
    n_used = (ends[-1] // tile).astype(jnp.int32).reshape(1)
    tile_start = jnp.arange(nt, dtype=jnp.int32) * tile
    tile_expert = jnp.sum((tile_start[:, None] >= ends[None, :]).astype(jnp.int32), axis=1)
    last_expert = jnp.sum((jnp.maximum(ends[-1] - tile, 0) >= ends).astype(jnp.int32))
    tile_expert = jnp.where(tile_start < ends[-1], tile_expert, last_expert).astype(jnp.int32)
    pos2 = pos.reshape(n, TOP_K)
    return src_tok, row_w.reshape(slots, 1), tile_expert, n_used, pos2[:, 0], pos2[:, 1]


def _final_norm_kernel(x_ref, g_ref, o_ref):
    x = x_ref[...]
    o_ref[...] = x * lax.rsqrt(jnp.mean(x * x, axis=-1, keepdims=True) + EPS) * g_ref[...]


def final_norm(x2, g, tm=512):
    n, d = x2.shape
    return pl.pallas_call(
        _final_norm_kernel,
        grid=(n // tm,),
        in_specs=[pl.BlockSpec((tm, d), lambda i: (i, 0)), pl.BlockSpec((1, d), lambda i: (0, 0))],
        out_specs=pl.BlockSpec((tm, d), lambda i: (i, 0)),
        out_shape=jax.ShapeDtypeStruct((n, d), F32),
        compiler_params=_cparams("parallel"),
        name="final_norm",
    )(x2, g.reshape(1, d).astype(F32))


MOE_TILE = 512


def kernel(x, c, ada_w, ada_b, norm1_g, norm2_g, ev_w_in, s5_log_dt, s5_lambda_re, s5_lambda_im, s5_b_re, s5_b_im, s5_c_re, s5_c_im, s5_d, s5_glu_w, s5_glu_b, ev_w_out, od_w_in, conf_dw, conf_dw_b, conf_ln_g, conf_ln_b, gdn_conv, gdn_a_log, gdn_dt_bias, gdn_norm_g, od_w_out, ffn_w_gate, ffn_w_up, ffn_w_down, moe_router, moe_w_gate, moe_w_up, moe_w_down, final_norm_g):
    bsz, seq, d = x.shape
    n = bsz * seq
    depth = ada_w.shape[0]
    x2 = x.reshape(n, d).astype(F32)
    mod = adaln(c.astype(F32), ada_w, ada_b)

    for layer in range(depth):
        i = layer // 2
        m = mod[layer]
        shift1, scale1, gate1, shift2, scale2, gate2 = (m[:, k * d:(k + 1) * d][:, None, :] for k in range(6))
        g1 = norm1_g[layer].astype(F32)
        g2 = norm2_g[layer].astype(F32)
        if layer % 2 == 0:
            p = inproj(x2, g1, scale1, shift1, ev_w_in[i].astype(BF16), seq, tm=1024, tn=512)
            p3 = p.reshape(bsz, seq, -1)
            consts = _s5_constants(s5_log_dt[i], s5_lambda_re[i], s5_lambda_im[i], s5_b_re[i], s5_b_im[i],
                                   s5_c_re[i], s5_c_im[i])
            y_a = s5_mixer(p3, consts, s5_d[i], s5_glu_w[i], s5_glu_b[i])
            y_b = dilated_attention(p3, S5_WIDTH)
            w_out = ev_w_out[i].astype(BF16)
            x2 = outproj(y_a.reshape(n, -1), y_b.reshape(n, -1), w_out[:S5_WIDTH], w_out[S5_WIDTH:],
                         x2, gate1, seq)
            x2 = ffn(x2, g2, scale2, shift2, gate2, ffn_w_gate[i].astype(BF16), ffn_w_up[i].astype(BF16),
                     ffn_w_down[i].astype(BF16), seq)
        else:
            w_in = od_w_in[i]
            main = 2 * CONV_WIDTH + 4 * GDN_WIDTH
            p = inproj(x2, g1, scale1, shift1, w_in[:, :main].astype(BF16), seq, tm=1024, tn=512)
            w_gates = jnp.zeros((d, LANES), F32).at[:, :2 * GDN_HEADS].set(w_in[:, main:]).astype(BF16)
            pg = inproj(x2, g1, scale1, shift1, w_gates, seq, tm=1024, tn=LANES)
            p3 = p.reshape(bsz, seq, -1)
            y_c = conformer_conv(p3, conf_dw[i], conf_dw_b[i], conf_ln_g[i], conf_ln_b[i])
            q, k, v, gates = gdn_prep(p3, pg, gdn_conv[i], gdn_a_log[i], gdn_dt_bias[i], 2 * CONV_WIDTH)
            y_d = gdn_core(q, k, v, p3, 2 * CONV_WIDTH + 3 * GDN_WIDTH, gates, gdn_norm_g[i])
            w_out = od_w_out[i].astype(BF16)
            x2 = outproj(y_c.reshape(n, -1), y_d.reshape(n, -1), w_out[:CONV_WIDTH], w_out[CONV_WIDTH:],
                         x2, gate1, seq)
            h2, rt = router(x2, g2, scale2, shift2, moe_router[i], seq)
            src_tok, row_w, tile_expert, n_used, pos0, pos1 = _dispatch_plan(rt, MOE_TILE)
            xs = dispatch(h2, src_tok, n_used, MOE_TILE)
            ys = moe_ffn(xs, row_w, tile_expert, n_used, moe_w_gate[i].astype(BF16),
                         moe_w_up[i].astype(BF16), moe_w_down[i].astype(BF16), MOE_TILE)
            x2 = combine(ys, pos0, pos1, x2, gate2, seq)
    return final_norm(x2, final_norm_g).reshape(bsz, seq, d)
```
